```python
import jax, jax.numpy as jnp
from jax import lax
import numpy as np

D_MODEL = 4096
BATCH = 2
SEQ = 8192
DEPTH = 4

HEAD_DIM = 128
ROPE_THETA = 500000.0
ROPE_DIM = HEAD_DIM // 4
NORM_EPS = 1e-6
SUBLN_EPS = 1e-5
DIL_PATTERNS = ((128, 1), (512, 4), (2048, 16))
DIL_GROUPS = len(DIL_PATTERNS)
DIL_HEADS_PER_GROUP = 10
DIL_HEADS = DIL_GROUPS * DIL_HEADS_PER_GROUP
DIL_WIDTH = DIL_HEADS * HEAD_DIM
BAND_BLOCK = 128
DIFF_HEADS = D_MODEL // (2 * HEAD_DIM)
DIFF_WIDTH = DIFF_HEADS * 2 * HEAD_DIM
QUERY_BLOCK = 128
D_FF = 5632
N_DIL_LAYERS = (DEPTH + 1) // 2
N_DIFF_LAYERS = DEPTH // 2

kernel_name = "hybrid_dilated_diff_macaron_sandwich"


def rms_norm(x, g, eps=NORM_EPS):
    x32 = x.astype(jnp.float32)
    y = x32 * lax.rsqrt(jnp.mean(x32 * x32, axis=-1, keepdims=True) + eps)
    return (y * g.astype(jnp.float32)).astype(x.dtype)


def rope_tables(positions):
    inv_freq = 1.0 / (ROPE_THETA ** (jnp.arange(0, ROPE_DIM, 2, dtype=jnp.float32) / ROPE_DIM))
    ang = positions.astype(jnp.float32)[..., None] * inv_freq
    return jnp.cos(ang)[:, :, None, :], jnp.sin(ang)[:, :, None, :]


def apply_partial_rope(t, cos, sin):
    t32 = t.astype(jnp.float32)
    half = ROPE_DIM // 2
    t1, t2, rest = t32[..., :half], t32[..., half:ROPE_DIM], t32[..., ROPE_DIM:]
    out = jnp.concatenate([t1 * cos - t2 * sin, t2 * cos + t1 * sin, rest], axis=-1)
    return out.astype(t.dtype)


def swiglu(h, w_in, w_out):
    gate, up = jnp.split(h @ w_in, 2, axis=-1)
    return (jax.nn.silu(gate) * up) @ w_out


def dilated_group_attention(q, k, v, dilation, steps):
    B, S, H, hd = q.shape
    L = S // dilation

    def stride_gather(t):
        return t.reshape(B, L, dilation, H, hd).transpose(0, 3, 2, 1, 4)

    Lp = -(-L // BAND_BLOCK) * BAND_BLOCK
    nb = Lp // BAND_BLOCK
    padw = ((0, 0), (0, 0), (0, 0), (0, Lp - L), (0, 0))
    qb, kb, vb = [jnp.pad(stride_gather(t), padw).reshape(B, H, dilation, nb, BAND_BLOCK, hd)
                  for t in (q, k, v)]

    def with_prev(t):
        prev = jnp.pad(t[:, :, :, :-1], ((0, 0), (0, 0), (0, 0), (1, 0), (0, 0), (0, 0)))
        return jnp.concatenate([prev, t], axis=4)

    kk, vv = with_prev(kb), with_prev(vb)
    s = jnp.einsum('bhrnqd,bhrnkd->bhrnqk', qb, kk,
                   preferred_element_type=jnp.float32) * (hd ** -0.5)
    qi = jnp.arange(BAND_BLOCK)[:, None]
    kj = jnp.arange(2 * BAND_BLOCK)[None, :]
    back = qi + BAND_BLOCK - kj
    band = (back >= 0) & (back <= steps)
    has_prev = (jnp.arange(nb) > 0)[:, None, None] | (kj >= BAND_BLOCK)[None]
    mask = band[None] & has_prev
    s = jnp.where(mask, s, -jnp.inf)
    m = jnp.max(s, axis=-1, keepdims=True)
    p = jnp.exp(s - m)
    denom = jnp.sum(p, axis=-1, keepdims=True)
    o = jnp.einsum('bhrnqk,bhrnkd->bhrnqd', (p / denom).astype(v.dtype), vv)
    lse = (m + jnp.log(denom))[..., 0]
    o = o.reshape(B, H, dilation, Lp, hd)[:, :, :, :L].transpose(0, 3, 2, 1, 4).reshape(B, S, H, hd)
    lse = lse.reshape(B, H, dilation, Lp)[..., :L].transpose(0, 3, 2, 1).reshape(B, S, H)
    return o, lse


def dilated_mixer(h, w_in, w_out, cos, sin):
    B, S, _ = h.shape
    q, k, v = jnp.split(h @ w_in, 3, axis=-1)
    q = apply_partial_rope(q.reshape(B, S, DIL_HEADS, HEAD_DIM), cos, sin)
    k = apply_partial_rope(k.reshape(B, S, DIL_HEADS, HEAD_DIM), cos, sin)
    v = v.reshape(B, S, DIL_HEADS, HEAD_DIM)
    outs, lses = [], []
    for g, (window, dil) in enumerate(DIL_PATTERNS):
        sl = slice(g * DIL_HEADS_PER_GROUP, (g + 1) * DIL_HEADS_PER_GROUP)
        o, lse = dilated_group_attention(q[:, :, sl], k[:, :, sl], v[:, :, sl], dil, window // dil)
        outs.append(o)
        lses.append(lse)
    alpha = jax.nn.softmax(jnp.stack(lses, axis=0), axis=0)
    o = jnp.concatenate([alpha[g][..., None].astype(outs[g].dtype) * outs[g]
                         for g in range(DIL_GROUPS)], axis=2)
    return o.reshape(B, S, DIL_WIDTH) @ w_out


def diff_mixer(h, w_in, lam_params, subln_g, w_out, cos, sin, layer_idx):
    B, S, _ = h.shape
    nb = S // QUERY_BLOCK
    q, k, v = jnp.split(h @ w_in, 3, axis=-1)
    q = apply_partial_rope(q.reshape(B, S, 2 * DIFF_HEADS, HEAD_DIM), cos, sin)
    k = apply_partial_rope(k.reshape(B, S, 2 * DIFF_HEADS, HEAD_DIM), cos, sin)
    q = q.reshape(B, nb, QUERY_BLOCK, DIFF_HEADS, 2, HEAD_DIM).transpose(1, 0, 3, 4, 2, 5)
    k = k.reshape(B, S, DIFF_HEADS, 2, HEAD_DIM).transpose(0, 2, 3, 1, 4)
    v = v.reshape(B, S, DIFF_HEADS, 2 * HEAD_DIM).transpose(0, 2, 1, 3)
    lam_init = 0.8 - 0.6 * float(np.exp(-0.3 * layer_idx))
    lp = lam_params.astype(jnp.float32)
    lam = (jnp.exp(jnp.sum(lp[0] * lp[1])) - jnp.exp(jnp.sum(lp[2] * lp[3])) + lam_init)
    kpos = jnp.arange(S)

    def block(args):
        qblk, start = args
        s = jnp.einsum('bhcqd,bhckd->bhcqk', qblk, k,
                       preferred_element_type=jnp.float32) * (HEAD_DIM ** -0.5)
        qpos = start + jnp.arange(QUERY_BLOCK)
        s = jnp.where(kpos[None, :] <= qpos[:, None], s, -jnp.inf)
        p = jax.nn.softmax(s, axis=-1)
        a = p[:, :, 0] - lam * p[:, :, 1]
        return jnp.einsum('bhqk,bhke->bhqe', a.astype(v.dtype), v)

    starts = jnp.arange(nb, dtype=jnp.int32) * QUERY_BLOCK
    o = lax.map(block, (q, starts))
    o = o.transpose(1, 0, 3, 2, 4).reshape(B, S, DIFF_HEADS, 2 * HEAD_DIM)
    o = rms_norm(o, subln_g, SUBLN_EPS) * (1.0 - lam_init)
    return o.reshape(B, S, DIFF_WIDTH) @ w_out


def setup_inputs(seed: int = 0) -> dict:
    key = jax.random.key(seed)
    ks = jax.random.split(key, 10)
    f32 = jnp.float32

    def dense(k, shape, fan_in):
        return jax.random.normal(k, shape, f32) * (fan_in ** -0.5)

    x = jax.random.normal(ks[0], (BATCH, SEQ, D_MODEL), f32)
    positions = jnp.broadcast_to(jnp.arange(SEQ, dtype=jnp.int32), (BATCH, SEQ))
    norm_g = 1.0 + 0.02 * jax.random.normal(ks[1], (DEPTH, 6, D_MODEL), f32)
    ffn_in = dense(ks[2], (DEPTH, 2, D_MODEL, 2 * D_FF), D_MODEL)
    ffn_out = dense(ks[3], (DEPTH, 2, D_FF, D_MODEL), D_FF)
    dil_w_in = dense(ks[4], (N_DIL_LAYERS, D_MODEL, 3 * DIL_WIDTH), D_MODEL)
    dil_w_out = dense(ks[5], (N_DIL_LAYERS, DIL_WIDTH, D_MODEL), DIL_WIDTH)
    diff_w_in = dense(ks[6], (N_DIFF_LAYERS, D_MODEL, 3 * DIFF_WIDTH), D_MODEL)
    diff_lambda = 0.1 * jax.random.normal(ks[7], (N_DIFF_LAYERS, 4, HEAD_DIM), f32)
    diff_subln_g = 1.0 + 0.02 * jax.random.normal(ks[8], (N_DIFF_LAYERS, 2 * HEAD_DIM), f32)
    diff_w_out = dense(ks[9], (N_DIFF_LAYERS, DIFF_WIDTH, D_MODEL), DIFF_WIDTH)
    return {"x": x, "positions": positions, "norm_g": norm_g, "ffn_in": ffn_in,
            "ffn_out": ffn_out, "dil_w_in": dil_w_in, "dil_w_out": dil_w_out,
            "diff_w_in": diff_w_in, "diff_lambda": diff_lambda,
            "diff_subln_g": diff_subln_g, "diff_w_out": diff_w_out}


def reference(x, positions, norm_g, ffn_in, ffn_out, dil_w_in, dil_w_out,
              diff_w_in, diff_lambda, diff_subln_g, diff_w_out):
    cos, sin = rope_tables(positions)
    for l in range(DEPTH):
        g = norm_g[l]
        x = x + 0.5 * rms_norm(swiglu(rms_norm(x, g[0]), ffn_in[l, 0], ffn_out[l, 0]), g[1])
        h = rms_norm(x, g[2])
        if l % 2 == 0:
            m = dilated_mixer(h, dil_w_in[l // 2], dil_w_out[l // 2], cos, sin)
        else:
            m = diff_mixer(h, diff_w_in[l // 2], diff_lambda[l // 2], diff_subln_g[l // 2],
                           diff_w_out[l // 2], cos, sin, l)
        x = x + rms_norm(m, g[3])
        x = x + 0.5 * rms_norm(swiglu(rms_norm(x, g[4]), ffn_in[l, 1], ffn_out[l, 1]), g[5])
    return x
```

```python
import functools

import numpy as np
import jax
import jax.numpy as jnp
from jax import lax
from jax.experimental import pallas as pl
from jax.experimental.pallas import tpu as pltpu

HEAD_DIM = 128
ROPE_THETA = 500000.0
ROPE_DIM = HEAD_DIM // 4
ROPE_HALF = ROPE_DIM // 2
NORM_EPS = 1e-6
SUBLN_EPS = 1e-5
DIL_PATTERNS = ((128, 1), (512, 4), (2048, 16))
DIL_GROUPS = len(DIL_PATTERNS)
DIL_HEADS_PER_GROUP = 10
DIL_HEADS = DIL_GROUPS * DIL_HEADS_PER_GROUP
DIL_WIDTH = DIL_HEADS * HEAD_DIM
DIL_GROUP_WIDTH = DIL_HEADS_PER_GROUP * HEAD_DIM
BAND_BLOCK = 128
LANES = 128

VMEM_LIMIT_BYTES = 56 * 1024 * 1024

F32 = jnp.float32
BF16 = jnp.bfloat16


def _params(*sem):
    return pltpu.CompilerParams(dimension_semantics=sem, vmem_limit_bytes=VMEM_LIMIT_BYTES)


def _rms(x, eps):
    return x * lax.rsqrt(jnp.mean(x * x, axis=-1, keepdims=True) + eps)


def _rms_cast_kernel(x_ref, g_ref, o_ref):
    o_ref[...] = (_rms(x_ref[...], NORM_EPS) * g_ref[...]).astype(o_ref.dtype)


def rms_cast(x, g, *, tm=256):
    M, D = x.shape
    return pl.pallas_call(
        _rms_cast_kernel,
        grid=(M // tm,),
        in_specs=[pl.BlockSpec((tm, D), lambda i: (i, 0)), pl.BlockSpec((1, D), lambda i: (0, 0))],
        out_specs=pl.BlockSpec((tm, D), lambda i: (i, 0)),
        out_shape=jax.ShapeDtypeStruct((M, D), BF16),
        compiler_params=_params("parallel"),
        name="rms_cast",
    )(x, g)


def _resid_kernel(x_ref, y_ref, gp_ref, gn_ref, xo_ref, ho_ref, *, coef):
    xn = x_ref[...] + coef * (_rms(y_ref[...], NORM_EPS) * gp_ref[...])
    xo_ref[...] = xn
    ho_ref[...] = (_rms(xn, NORM_EPS) * gn_ref[...]).astype(ho_ref.dtype)


def _resid_last_kernel(x_ref, y_ref, gp_ref, xo_ref, *, coef):
    xo_ref[...] = x_ref[...] + coef * (_rms(y_ref[...], NORM_EPS) * gp_ref[...])


def resid_norm(x, y, g_post, g_next, coef, *, tm=256):
    M, D = x.shape
    row = pl.BlockSpec((tm, D), lambda i: (i, 0))
    vec = pl.BlockSpec((1, D), lambda i: (0, 0))
    if g_next is None:
        return pl.pallas_call(
            functools.partial(_resid_last_kernel, coef=coef),
            grid=(M // tm,),
            in_specs=[row, row, vec],
            out_specs=row,
            out_shape=jax.ShapeDtypeStruct((M, D), F32),
            compiler_params=_params("parallel"),
            name="resid_last",
        )(x, y, g_post), None
    return pl.pallas_call(
        functools.partial(_resid_kernel, coef=coef),
        grid=(M // tm,),
        in_specs=[row, row, vec, vec],
        out_specs=[row, row],
        out_shape=[jax.ShapeDtypeStruct((M, D), F32), jax.ShapeDtypeStruct((M, D), BF16)],
        compiler_params=_params("parallel"),
        name="resid_norm",
    )(x, y, g_post, g_next)


def _mm_kernel(a_ref, w_ref, o_ref):
    o_ref[...] = jnp.dot(a_ref[...], w_ref[...], preferred_element_type=F32).astype(o_ref.dtype)


def matmul(a, w, *, tm, tn, out_dtype=F32):
    M, K = a.shape
    N = w.shape[1]
    return pl.pallas_call(
        _mm_kernel,
        grid=(M // tm, N // tn),
        in_specs=[pl.BlockSpec((tm, K), lambda i, j: (i, 0)), pl.BlockSpec((K, tn), lambda i, j: (0, j))],
        out_specs=pl.BlockSpec((tm, tn), lambda i, j: (i, j)),
        out_shape=jax.ShapeDtypeStruct((M, N), out_dtype),
        compiler_params=_params("parallel", "arbitrary"),
        name="matmul",
    )(a, w)


def _mm_swiglu_kernel(a_ref, wg_ref, wu_ref, o_ref):
    a = a_ref[...]
    gate = jnp.dot(a, wg_ref[...], preferred_element_type=F32)
    up = jnp.dot(a, wu_ref[...], preferred_element_type=F32)
    o_ref[...] = (gate * jax.nn.sigmoid(gate) * up).astype(o_ref.dtype)


def matmul_swiglu(a, w_in, *, tm, tn):
    M, K = a.shape
    F = w_in.shape[1] // 2
    nj = F // tn
    return pl.pallas_call(
        _mm_swiglu_kernel,
        grid=(M // tm, nj),
        in_specs=[
            pl.BlockSpec((tm, K), lambda i, j: (i, 0)),
            pl.BlockSpec((K, tn), lambda i, j: (0, j)),
            pl.BlockSpec((K, tn), lambda i, j: (0, j + nj)),
        ],
        out_specs=pl.BlockSpec((tm, tn), lambda i, j: (i, j)),
        out_shape=jax.ShapeDtypeStruct((M, F), BF16),
        compiler_params=_params("parallel", "arbitrary"),
        name="matmul_swiglu",
    )(a, w_in, w_in)


def _rope_table_kernel(pos_ref, invf_ref, c_ref, s1_ref, s2_ref):
    ang = pos_ref[...].astype(F32) * invf_ref[...]
    lane = lax.broadcasted_iota(jnp.int32, ang.shape, 1)
    cos = jnp.cos(ang)
    sin = jnp.sin(ang)
    c_ref[...] = jnp.where(lane < ROPE_DIM, cos, 1.0)
    s1_ref[...] = jnp.where(lane < ROPE_HALF, -sin, 0.0)
    s2_ref[...] = jnp.where((lane >= ROPE_HALF) & (lane < ROPE_DIM), sin, 0.0)


def rope_tables(positions, *, tm=512):
    M = positions.size
    inv_freq = 1.0 / (ROPE_THETA ** (jnp.arange(0, ROPE_DIM, 2, dtype=F32) / ROPE_DIM))
    invf = jnp.concatenate([inv_freq, inv_freq, jnp.zeros((LANES - ROPE_DIM,), F32)]).reshape(1, LANES)
    tab = jax.ShapeDtypeStruct((M, LANES), F32)
    row = pl.BlockSpec((tm, LANES), lambda i: (i, 0))
    return pl.pallas_call(
        _rope_table_kernel,
        grid=(M // tm,),
        in_specs=[pl.BlockSpec((tm, 1), lambda i: (i, 0)), pl.BlockSpec((1, LANES), lambda i: (0, 0))],
        out_specs=[row, row, row],
        out_shape=[tab, tab, tab],
        compiler_params=_params("parallel"),
        name="rope_tables",
    )(positions.reshape(M, 1), invf)


def _mm_rope_kernel(a_ref, w_ref, c_ref, s1_ref, s2_ref, o_ref, *, n_rope_tiles):
    acc = jnp.dot(a_ref[...], w_ref[...], preferred_element_type=F32)
    j = pl.program_id(1)

    @pl.when(j < n_rope_tiles)
    def _():
        c, s1, s2 = c_ref[...], s1_ref[...], s2_ref[...]
        for h in range(acc.shape[1] // HEAD_DIM):
            sl = slice(h * HEAD_DIM, (h + 1) * HEAD_DIM)
            t = acc[:, sl]
            fwd = pltpu.roll(t, HEAD_DIM - ROPE_HALF, 1)
            bwd = pltpu.roll(t, ROPE_HALF, 1)
            o_ref[:, sl] = (t * c + fwd * s1 + bwd * s2).astype(o_ref.dtype)

    @pl.when(j >= n_rope_tiles)
    def _():
        o_ref[...] = acc.astype(o_ref.dtype)


def matmul_rope(a, w, tabs, *, rope_cols, tm, tn):
    M, K = a.shape
    N = w.shape[1]
    tab = pl.BlockSpec((tm, LANES), lambda i, j: (i, 0))
    return pl.pallas_call(
        functools.partial(_mm_rope_kernel, n_rope_tiles=rope_cols // tn),
        grid=(M // tm, N // tn),
        in_specs=[pl.BlockSpec((tm, K), lambda i, j: (i, 0)), pl.BlockSpec((K, tn), lambda i, j: (0, j)),
                  tab, tab, tab],
        out_specs=pl.BlockSpec((tm, tn), lambda i, j: (i, j)),
        out_shape=jax.ShapeDtypeStruct((M, N), BF16),
        compiler_params=_params("parallel", "arbitrary"),
        name="matmul_rope",
    )(a, w, *tabs)


_NT = (((1,), (1,)), ((), ()))


def _dil_attn_kernel(q_ref, kp_ref, kc_ref, vp_ref, vc_ref, o_ref, lse_ref):
    n = pl.program_id(2)
    scale = HEAD_DIM ** -0.5
    qi = lax.broadcasted_iota(jnp.int32, (BAND_BLOCK, BAND_BLOCK), 0)
    kj = lax.broadcasted_iota(jnp.int32, (BAND_BLOCK, BAND_BLOCK), 1)
    mask_cur = kj <= qi
    mask_prev = (kj >= qi) & (n > 0)
    lane = lax.broadcasted_iota(jnp.int32, (BAND_BLOCK, LANES), 1)
    lse_tile = jnp.zeros((BAND_BLOCK, LANES), F32)
    for j in range(DIL_HEADS_PER_GROUP):
        sl = slice(j * HEAD_DIM, (j + 1) * HEAD_DIM)
        q = q_ref[0, :, sl]
        s_c = lax.dot_general(q, kc_ref[0, :, sl], _NT, preferred_element_type=F32) * scale
        s_p = lax.dot_general(q, kp_ref[0, :, sl], _NT, preferred_element_type=F32) * scale
        s_c = jnp.where(mask_cur, s_c, -jnp.inf)
        s_p = jnp.where(mask_prev, s_p, -jnp.inf)
        m = jnp.maximum(jnp.max(s_c, axis=-1, keepdims=True), jnp.max(s_p, axis=-1, keepdims=True))
        p_c = jnp.exp(s_c - m)
        p_p = jnp.exp(s_p - m)
        den = jnp.sum(p_c, axis=-1, keepdims=True) + jnp.sum(p_p, axis=-1, keepdims=True)
        o = (jnp.dot(p_c.astype(BF16), vc_ref[0, :, sl], preferred_element_type=F32)
             + jnp.dot(p_p.astype(BF16), vp_ref[0, :, sl], preferred_element_type=F32))
        o_ref[0, :, sl] = o / den
        lse_tile = jnp.where(lane == j, m + jnp.log(den), lse_tile)
    lse_ref[0] = lse_tile


def dilated_group_attention(qkv, group, dilation, batch):
    M = qkv.shape[0]
    S = M // batch
    L = S // dilation
    nb = L // BAND_BLOCK
    per_pos = 3 * DIL_GROUPS
    view = qkv.reshape(batch, L, dilation * 3 * DIL_WIDTH)
    blk = (1, BAND_BLOCK, DIL_GROUP_WIDTH)

    def spec(part, prev):
        def index(b, r, n):
            row = jnp.maximum(n - 1, 0) if prev else n
            return (b, row, r * per_pos + part * DIL_GROUPS + group)
        return pl.BlockSpec(blk, index)

    o, lse = pl.pallas_call(
        _dil_attn_kernel,
        grid=(batch, dilation, nb),
        in_specs=[spec(0, False), spec(1, True), spec(1, False), spec(2, True), spec(2, False)],
        out_specs=[pl.BlockSpec(blk, lambda b, r, n: (b, n, r)),
                   pl.BlockSpec((1, BAND_BLOCK, LANES), lambda b, r, n: (b, n, r))],
        out_shape=[jax.ShapeDtypeStruct((batch, L, dilation * DIL_GROUP_WIDTH), F32),
                   jax.ShapeDtypeStruct((batch, L, dilation * LANES), F32)],
        compiler_params=_params("parallel", "parallel", "arbitrary"),
        name=f"dil_attn_g{group}",
    )(view, view, view, view, view)
    return o.reshape(M, DIL_GROUP_WIDTH), lse.reshape(M, LANES)


def _dil_mix_kernel(o0_ref, o1_ref, o2_ref, l0_ref, l1_ref, l2_ref, out_ref):
    lses = [l0_ref[...], l1_ref[...], l2_ref[...]]
    mx = jnp.maximum(jnp.maximum(lses[0], lses[1]), lses[2])
    es = [jnp.exp(l - mx) for l in lses]
    tot = es[0] + es[1] + es[2]
    for g, o_ref in enumerate((o0_ref, o1_ref, o2_ref)):
        alpha = es[g] / tot
        for j in range(DIL_HEADS_PER_GROUP):
            src = slice(j * HEAD_DIM, (j + 1) * HEAD_DIM)
            col = (g * DIL_HEADS_PER_GROUP + j) * HEAD_DIM
            out_ref[:, col:col + HEAD_DIM] = (alpha[:, j:j + 1] * o_ref[:, src]).astype(out_ref.dtype)


def dilated_mix(outs, lses, *, tm=256):
    M = outs[0].shape[0]
    o_spec = pl.BlockSpec((tm, DIL_GROUP_WIDTH), lambda i: (i, 0))
    l_spec = pl.BlockSpec((tm, LANES), lambda i: (i, 0))
    return pl.pallas_call(
        _dil_mix_kernel,
        grid=(M // tm,),
        in_specs=[o_spec] * 3 + [l_spec] * 3,
        out_specs=pl.BlockSpec((tm, DIL_WIDTH), lambda i: (i, 0)),
        out_shape=jax.ShapeDtypeStruct((M, DIL_WIDTH), BF16),
        compiler_params=_params("parallel"),
        name="dil_mix",
    )(*outs, *lses)


def _diff_attn_kernel(lam_ref, q_ref, k_ref, v_ref, g_ref, o_ref, m_sc, l_sc, acc_sc, *, tq, tk, lam_init):
    qi = pl.program_id(2)
    scale = HEAD_DIM ** -0.5
    m_sc[...] = jnp.full(m_sc.shape, -jnp.inf, F32)
    l_sc[...] = jnp.zeros(l_sc.shape, F32)
    acc_sc[...] = jnp.zeros(acc_sc.shape, F32)

    def step(kb, masked):
        k0 = pl.multiple_of(kb * tk, tk)
        v = v_ref[0, pl.ds(k0, tk), :]
        for c in range(2):
            cs = slice(c * HEAD_DIM, (c + 1) * HEAD_DIM)
            s = lax.dot_general(q_ref[0, :, cs], k_ref[0, pl.ds(k0, tk), cs], _NT,
                                preferred_element_type=F32) * scale
            if masked:
                qpos = qi * tq + lax.broadcasted_iota(jnp.int32, (tq, tk), 0)
                kpos = k0 + lax.broadcasted_iota(jnp.int32, (tq, tk), 1)
                s = jnp.where(kpos <= qpos, s, -jnp.inf)
            m_prev = m_sc[c]
            m_new = jnp.maximum(m_prev, jnp.max(s, axis=-1, keepdims=True))
            alpha = jnp.exp(m_prev - m_new)
            p = jnp.exp(s - m_new)
            l_sc[c] = alpha * l_sc[c] + jnp.sum(p, axis=-1, keepdims=True)
            acc_sc[c] = alpha * acc_sc[c] + jnp.dot(p.astype(BF16), v, preferred_element_type=F32)
            m_sc[c] = m_new

    blocks_per_q = tq // tk

    def full_step(kb, carry):
        step(kb, False)
        return carry

    lax.fori_loop(0, qi * blocks_per_q, full_step, 0)
    for d in range(blocks_per_q):
        step(qi * blocks_per_q + d, True)

    lp = lam_ref[...]
    lam = (jnp.exp(jnp.sum(lp[0:1] * lp[1:2], axis=-1, keepdims=True))
           - jnp.exp(jnp.sum(lp[2:3] * lp[3:4], axis=-1, keepdims=True)) + lam_init)
    o = acc_sc[0] / l_sc[0] - lam * (acc_sc[1] / l_sc[1])
    o_ref[0] = (_rms(o, SUBLN_EPS) * g_ref[...] * (1.0 - lam_init)).astype(o_ref.dtype)


def diff_attention(qkv, lam_params, subln_g, batch, layer_idx, *, tq=512, tk=512):
    M, W3 = qkv.shape
    W = W3 // 3
    S = M // batch
    heads = W // (2 * HEAD_DIM)
    hw = 2 * HEAD_DIM
    lam_init = 0.8 - 0.6 * float(np.exp(-0.3 * layer_idx))
    view = qkv.reshape(batch, S, W3)
    out = pl.pallas_call(
        functools.partial(_diff_attn_kernel, tq=tq, tk=tk, lam_init=lam_init),
        grid=(batch, heads, S // tq),
        in_specs=[
            pl.BlockSpec((4, HEAD_DIM), lambda b, h, i: (0, 0)),
            pl.BlockSpec((1, tq, hw), lambda b, h, i: (b, i, h)),
            pl.BlockSpec((1, S, hw), lambda b, h, i: (b, 0, heads + h)),
            pl.BlockSpec((1, S, hw), lambda b, h, i: (b, 0, 2 * heads + h)),
            pl.BlockSpec((1, hw), lambda b, h, i: (0, 0)),
        ],
        out_specs=pl.BlockSpec((1, tq, hw), lambda b, h, i: (b, i, h)),
        out_shape=jax.ShapeDtypeStruct((batch, S, W), BF16),
        scratch_shapes=[pltpu.VMEM((2, tq, 1), F32), pltpu.VMEM((2, tq, 1), F32),
                        pltpu.VMEM((2, tq, hw), F32)],
        compiler_params=_params("parallel", "parallel", "arbitrary"),
        name="diff_attn",
    )(lam_params, view, view, view, subln_g.reshape(1, hw))
    return out.reshape(M, W)


def _ffn(x, h, w_in, w_out, g_post, g_next):
    act = matmul_swiglu(h, w_in, tm=1024, tn=512)
    y = matmul(act, w_out, tm=1024, tn=512)
    return resid_norm(x, y, g_post, g_next, 0.5)


def kernel(x, positions, norm_g, ffn_in, ffn_out, dil_w_in, dil_w_out, diff_w_in, diff_lambda,
           diff_subln_g, diff_w_out):
    B, S, D = x.shape
    depth = norm_g.shape[0]
    M = B * S
    x = x.reshape(M, D)
    g = norm_g.reshape(depth * 6, 1, D)
    ffn_in, ffn_out, dil_w_in, dil_w_out, diff_w_in, diff_w_out = (
        w.astype(BF16) for w in (ffn_in, ffn_out, dil_w_in, dil_w_out, diff_w_in, diff_w_out))
    tabs = rope_tables(positions)

    h = rms_cast(x, g[0])
    for l in range(depth):
        gl = g[6 * l:6 * l + 6]
        x, h = _ffn(x, h, ffn_in[l, 0], ffn_out[l, 0], gl[1], gl[2])
        if l % 2 == 0:
            qkv = matmul_rope(h, dil_w_in[l // 2], tabs, rope_cols=2 * DIL_WIDTH, tm=1024, tn=768)
            parts = [dilated_group_attention(qkv, grp, dil, B) for grp, (_, dil) in enumerate(DIL_PATTERNS)]
            mixed = dilated_mix([p[0] for p in parts], [p[1] for p in parts])
            y = matmul(mixed, dil_w_out[l // 2], tm=1024, tn=512)
        else:
            w_in = diff_w_in[l // 2]
            qkv = matmul_rope(h, w_in, tabs, rope_cols=2 * (w_in.shape[1] // 3), tm=1024, tn=1024)
            att = diff_attention(qkv, diff_lambda[l // 2], diff_subln_g[l // 2], B, l)
            y = matmul(att, diff_w_out[l // 2], tm=1024, tn=512)
        x, h = resid_norm(x, y, gl[3], gl[4], 1.0)
        g_next = g[6 * l + 6] if l + 1 < depth else None
        x, h = _ffn(x, h, ffn_in[l, 1], ffn_out[l, 1], gl[5], g_next)
    return x.reshape(B, S, D)
```

```python
import functools

import numpy as np
import jax
import jax.numpy as jnp
from jax import lax
from jax.experimental import pallas as pl
from jax.experimental.pallas import tpu as pltpu

HEAD_DIM = 128
ROPE_THETA = 500000.0
ROPE_DIM = HEAD_DIM // 4
ROPE_HALF = ROPE_DIM // 2
NORM_EPS = 1e-6
SUBLN_EPS = 1e-5
DIL_PATTERNS = ((128, 1), (512, 4), (2048, 16))
DIL_GROUPS = len(DIL_PATTERNS)
DIL_HEADS_PER_GROUP = 10
DIL_HEADS = DIL_GROUPS * DIL_HEADS_PER_GROUP
DIL_WIDTH = DIL_HEADS * HEAD_DIM
DIL_GROUP_WIDTH = DIL_HEADS_PER_GROUP * HEAD_DIM
BAND_BLOCK = 128
LANES = 128
LN2 = float(np.log(2.0))
Q_PRESCALE = float(HEAD_DIM ** -0.5 / np.log(2.0))
assert all(window // dil == BAND_BLOCK for window, dil in DIL_PATTERNS)

VMEM_LIMIT_BYTES = 56 * 1024 * 1024

F32 = jnp.float32
BF16 = jnp.bfloat16


def _params(*sem):
    return pltpu.CompilerParams(dimension_semantics=sem, vmem_limit_bytes=VMEM_LIMIT_BYTES)


def _rms(x, eps):
    return x * lax.rsqrt(jnp.mean(x * x, axis=-1, keepdims=True) + eps)


def _rms_cast_kernel(x_ref, g_ref, o_ref):
    o_ref[...] = (_rms(x_ref[...], NORM_EPS) * g_ref[...]).astype(o_ref.dtype)


def rms_cast(x, g, *, tm=256):
    M, D = x.shape
    return pl.pallas_call(
        _rms_cast_kernel,
        grid=(M // tm,),
        in_specs=[pl.BlockSpec((tm, D), lambda i: (i, 0)), pl.BlockSpec((1, D), lambda i: (0, 0))],
        out_specs=pl.BlockSpec((tm, D), lambda i: (i, 0)),
        out_shape=jax.ShapeDtypeStruct((M, D), BF16),
        compiler_params=_params("parallel"),
        name="rms_cast",
    )(x, g)


def _resid_kernel(x_ref, y_ref, gp_ref, gn_ref, xo_ref, ho_ref, *, coef):
    xn = x_ref[...] + coef * (_rms(y_ref[...], NORM_EPS) * gp_ref[...])
    xo_ref[...] = xn
    ho_ref[...] = (_rms(xn, NORM_EPS) * gn_ref[...]).astype(ho_ref.dtype)


def _resid_last_kernel(x_ref, y_ref, gp_ref, xo_ref, *, coef):
    xo_ref[...] = x_ref[...] + coef * (_rms(y_ref[...], NORM_EPS) * gp_ref[...])


def resid_norm(x, y, g_post, g_next, coef, *, tm=256):
    M, D = x.shape
    row = pl.BlockSpec((tm, D), lambda i: (i, 0))
    vec = pl.BlockSpec((1, D), lambda i: (0, 0))
    if g_next is None:
        return pl.pallas_call(
            functools.partial(_resid_last_kernel, coef=coef),
            grid=(M // tm,),
            in_specs=[row, row, vec],
            out_specs=row,
            out_shape=jax.ShapeDtypeStruct((M, D), F32),
            compiler_params=_params("parallel"),
            name="resid_last",
        )(x, y, g_post), None
    return pl.pallas_call(
        functools.partial(_resid_kernel, coef=coef),
        grid=(M // tm,),
        in_specs=[row, row, vec, vec],
        out_specs=[row, row],
        out_shape=[jax.ShapeDtypeStruct((M, D), F32), jax.ShapeDtypeStruct((M, D), BF16)],
        compiler_params=_params("parallel"),
        name="resid_norm",
    )(x, y, g_post, g_next)


def _mm_kernel(a_ref, w_ref, o_ref):
    o_ref[...] = jnp.dot(a_ref[...], w_ref[...], preferred_element_type=F32).astype(o_ref.dtype)


def _weight_spec(w, widx, tn, col=lambda j: j):
    K = w.shape[-2]
    return pl.BlockSpec((None,) * len(widx) + (K, tn), lambda i, j: (*widx, 0, col(j)))


def matmul(a, w, widx, *, tm, tn, out_dtype=F32):
    M, K = a.shape
    N = w.shape[-1]
    return pl.pallas_call(
        _mm_kernel,
        grid=(M // tm, N // tn),
        in_specs=[pl.BlockSpec((tm, K), lambda i, j: (i, 0)), _weight_spec(w, widx, tn)],
        out_specs=pl.BlockSpec((tm, tn), lambda i, j: (i, j)),
        out_shape=jax.ShapeDtypeStruct((M, N), out_dtype),
        compiler_params=_params("parallel", "arbitrary"),
        name="matmul",
    )(a, w)


def _mm_swiglu_kernel(a_ref, wg_ref, wu_ref, o_ref):
    a = a_ref[...]
    gate = jnp.dot(a, wg_ref[...], preferred_element_type=F32)
    up = jnp.dot(a, wu_ref[...], preferred_element_type=F32)
    o_ref[...] = (gate * jax.nn.sigmoid(gate) * up).astype(o_ref.dtype)


def matmul_swiglu(a, w_in, widx, *, tm, tn):
    M, K = a.shape
    F = w_in.shape[-1] // 2
    nj = F // tn
    return pl.pallas_call(
        _mm_swiglu_kernel,
        grid=(M // tm, nj),
        in_specs=[
            pl.BlockSpec((tm, K), lambda i, j: (i, 0)),
            _weight_spec(w_in, widx, tn),
            _weight_spec(w_in, widx, tn, lambda j: j + nj),
        ],
        out_specs=pl.BlockSpec((tm, tn), lambda i, j: (i, j)),
        out_shape=jax.ShapeDtypeStruct((M, F), BF16),
        compiler_params=_params("parallel", "arbitrary"),
        name="matmul_swiglu",
    )(a, w_in, w_in)


def _rope_table_kernel(pos_ref, invf_ref, c_ref, s1_ref, s2_ref):
    ang = pos_ref[...].astype(F32) * invf_ref[...]
    lane = lax.broadcasted_iota(jnp.int32, ang.shape, 1)
    cos = jnp.cos(ang)
    sin = jnp.sin(ang)
    c_ref[...] = jnp.where(lane < ROPE_DIM, cos, 1.0)
    s1_ref[...] = jnp.where(lane < ROPE_HALF, -sin, 0.0)
    s2_ref[...] = jnp.where((lane >= ROPE_HALF) & (lane < ROPE_DIM), sin, 0.0)


def rope_tables(positions, *, tm=512):
    M = positions.size
    inv_freq = 1.0 / (ROPE_THETA ** (jnp.arange(0, ROPE_DIM, 2, dtype=F32) / ROPE_DIM))
    invf = jnp.concatenate([inv_freq, inv_freq, jnp.zeros((LANES - ROPE_DIM,), F32)]).reshape(1, LANES)
    tab = jax.ShapeDtypeStruct((M, LANES), F32)
    row = pl.BlockSpec((tm, LANES), lambda i: (i, 0))
    return pl.pallas_call(
        _rope_table_kernel,
        grid=(M // tm,),
        in_specs=[pl.BlockSpec((tm, 1), lambda i: (i, 0)), pl.BlockSpec((1, LANES), lambda i: (0, 0))],
        out_specs=[row, row, row],
        out_shape=[tab, tab, tab],
        compiler_params=_params("parallel"),
        name="rope_tables",
    )(positions.reshape(M, 1), invf)


def _mm_rope_kernel(a_ref, w_ref, c_ref, s1_ref, s2_ref, o_ref, *, n_rope_tiles, n_q_tiles):
    acc = jnp.dot(a_ref[...], w_ref[...], preferred_element_type=F32)
    j = pl.program_id(1)

    @pl.when(j < n_rope_tiles)
    def _():
        mult = jnp.where(j < n_q_tiles, Q_PRESCALE, 1.0)
        c, s1, s2 = c_ref[...] * mult, s1_ref[...] * mult, s2_ref[...] * mult
        for h in range(acc.shape[1] // HEAD_DIM):
            sl = slice(h * HEAD_DIM, (h + 1) * HEAD_DIM)
            t = acc[:, sl]
            fwd = pltpu.roll(t, HEAD_DIM - ROPE_HALF, 1)
            bwd = pltpu.roll(t, ROPE_HALF, 1)
            o_ref[:, sl] = (t * c + fwd * s1 + bwd * s2).astype(o_ref.dtype)

    @pl.when(j >= n_rope_tiles)
    def _():
        o_ref[...] = acc.astype(o_ref.dtype)


def matmul_rope(a, w, widx, tabs, *, tm, tn):
    M, K = a.shape
    N = w.shape[-1]
    width = N // 3
    tab = pl.BlockSpec((tm, LANES), lambda i, j: (i, 0))
    return pl.pallas_call(
        functools.partial(_mm_rope_kernel, n_rope_tiles=2 * width // tn, n_q_tiles=width // tn),
        grid=(M // tm, N // tn),
        in_specs=[pl.BlockSpec((tm, K), lambda i, j: (i, 0)), _weight_spec(w, widx, tn), tab, tab, tab],
        out_specs=pl.BlockSpec((tm, tn), lambda i, j: (i, j)),
        out_shape=jax.ShapeDtypeStruct((M, N), BF16),
        compiler_params=_params("parallel", "arbitrary"),
        name="matmul_rope",
    )(a, w, *tabs)


_NT = (((1,), (1,)), ((), ()))


def _dil_attn_kernel(q_ref, kp_ref, kc_ref, vp_ref, vc_ref, o_ref, lse_ref):
    n = pl.program_id(2)
    qi = lax.broadcasted_iota(jnp.int32, (BAND_BLOCK, BAND_BLOCK), 0)
    kj = lax.broadcasted_iota(jnp.int32, (BAND_BLOCK, BAND_BLOCK), 1)
    mask_cur = kj <= qi
    mask_prev = (kj >= qi) & (n > 0)
    lane = lax.broadcasted_iota(jnp.int32, (BAND_BLOCK, LANES), 1)
    lse_tile = jnp.zeros((BAND_BLOCK, LANES), F32)
    for j in range(DIL_HEADS_PER_GROUP):
        sl = slice(j * HEAD_DIM, (j + 1) * HEAD_DIM)
        q = q_ref[0, :, sl]
        s_c = lax.dot_general(q, kc_ref[0, :, sl], _NT, preferred_element_type=F32)
        s_p = lax.dot_general(q, kp_ref[0, :, sl], _NT, preferred_element_type=F32)
        s_c = jnp.where(mask_cur, s_c, -jnp.inf)
        s_p = jnp.where(mask_prev, s_p, -jnp.inf)
        m = jnp.maximum(jnp.max(s_c, axis=-1, keepdims=True), jnp.max(s_p, axis=-1, keepdims=True))
        p_c = jnp.exp2(s_c - m)
        p_p = jnp.exp2(s_p - m)
        den = jnp.sum(p_c, axis=-1, keepdims=True) + jnp.sum(p_p, axis=-1, keepdims=True)
        o = (jnp.dot(p_c.astype(BF16), vc_ref[0, :, sl], preferred_element_type=F32)
             + jnp.dot(p_p.astype(BF16), vp_ref[0, :, sl], preferred_element_type=F32))
        o_ref[0, :, sl] = o / den
        lse_tile = jnp.where(lane == j, m * LN2 + jnp.log(den), lse_tile)
    lse_ref[0] = lse_tile


def dilated_group_attention(qkv, group, dilation, batch):
    M = qkv.shape[0]
    S = M // batch
    L = S // dilation
    nb = L // BAND_BLOCK
    per_pos = 3 * DIL_GROUPS
    view = qkv.reshape(batch, L, dilation * 3 * DIL_WIDTH)
    blk = (1, BAND_BLOCK, DIL_GROUP_WIDTH)

    def spec(part, prev):
        def index(b, r, n):
            row = jnp.maximum(n - 1, 0) if prev else n
            return (b, row, r * per_pos + part * DIL_GROUPS + group)
        return pl.BlockSpec(blk, index)

    o, lse = pl.pallas_call(
        _dil_attn_kernel,
        grid=(batch, dilation, nb),
        in_specs=[spec(0, False), spec(1, True), spec(1, False), spec(2, True), spec(2, False)],
        out_specs=[pl.BlockSpec(blk, lambda b, r, n: (b, n, r)),
                   pl.BlockSpec((1, BAND_BLOCK, LANES), lambda b, r, n: (b, n, r))],
        out_shape=[jax.ShapeDtypeStruct((batch, L, dilation * DIL_GROUP_WIDTH), F32),
                   jax.ShapeDtypeStruct((batch, L, dilation * LANES), F32)],
        compiler_params=_params("parallel", "parallel", "arbitrary"),
        name=f"dil_attn_g{group}",
    )(view, view, view, view, view)
    return o.reshape(M, DIL_GROUP_WIDTH), lse.reshape(M, LANES)


def _dil_mix_kernel(o0_ref, o1_ref, o2_ref, l0_ref, l1_ref, l2_ref, out_ref):
    lses = [l0_ref[...], l1_ref[...], l2_ref[...]]
    mx = jnp.maximum(jnp.maximum(lses[0], lses[1]), lses[2])
    es = [jnp.exp(l - mx) for l in lses]
    tot = es[0] + es[1] + es[2]
    for g, o_ref in enumerate((o0_ref, o1_ref, o2_ref)):
        alpha = es[g] / tot
        for j in range(DIL_HEADS_PER_GROUP):
            src = slice(j * HEAD_DIM, (j + 1) * HEAD_DIM)
            col = (g * DIL_HEADS_PER_GROUP + j) * HEAD_DIM
            out_ref[:, col:col + HEAD_DIM] = (alpha[:, j:j + 1] * o_ref[:, src]).astype(out_ref.dtype)


def dilated_mix(outs, lses, *, tm=256):
    M = outs[0].shape[0]
    o_spec = pl.BlockSpec((tm, DIL_GROUP_WIDTH), lambda i: (i, 0))
    l_spec = pl.BlockSpec((tm, LANES), lambda i: (i, 0))
    return pl.pallas_call(
        _dil_mix_kernel,
        grid=(M // tm,),
        in_specs=[o_spec] * 3 + [l_spec] * 3,
        out_specs=pl.BlockSpec((tm, DIL_WIDTH), lambda i: (i, 0)),
        out_shape=jax.ShapeDtypeStruct((M, DIL_WIDTH), BF16),
        compiler_params=_params("parallel"),
        name="dil_mix",
    )(*outs, *lses)


def _diff_attn_kernel(lam_ref, q_ref, k_ref, v_ref, g_ref, o_ref, qbd_sc, vT_sc, s0_sc, s1_sc, p0_sc, p1_sc,
                      a_sc, m_sc, l_sc, acc_sc, *, blk, lam_init):
    qi = pl.program_id(2)
    n_blocks = qi + 1
    s_sc = (s0_sc, s1_sc)
    p_sc = (p0_sc, p1_sc)

    @pl.when(qi == 0)
    def _():
        for kb in range(v_ref.shape[1] // blk):
            vT_sc[kb] = v_ref[0, kb * blk:(kb + 1) * blk, :].astype(F32).T.astype(BF16)
        qbd_sc[...] = jnp.zeros(qbd_sc.shape, BF16)

    qT = q_ref[0].astype(F32).T.astype(BF16)
    qbd_sc[0:HEAD_DIM, 0:blk] = qT[0:HEAD_DIM]
    qbd_sc[HEAD_DIM:2 * HEAD_DIM, blk:2 * blk] = qT[HEAD_DIM:2 * HEAD_DIM]
    m_sc[...] = jnp.full(m_sc.shape, -jnp.inf, F32)
    l_sc[...] = jnp.zeros(l_sc.shape, F32)
    acc_sc[...] = jnp.zeros(acc_sc.shape, F32)

    def key_block(t):
        return jnp.where(t == 0, qi, t - 1)

    def scores(t, slot, masked=False):
        k0 = pl.multiple_of(key_block(t) * blk, blk)
        sT = jnp.dot(k_ref[0, pl.ds(k0, blk), :], qbd_sc[...], preferred_element_type=F32)
        if masked:
            key = lax.broadcasted_iota(jnp.int32, sT.shape, 0)
            qry = lax.broadcasted_iota(jnp.int32, sT.shape, 1) & (blk - 1)
            sT = jnp.where(key <= qry, sT, -jnp.inf)
        s_sc[slot][...] = sT

    def softmax(slot):
        sT = s_sc[slot][...]
        m_prev = m_sc[...]
        m_new = jnp.maximum(m_prev, jnp.max(sT, axis=0, keepdims=True))
        alpha = jnp.exp2(m_prev - m_new)
        pT = jnp.exp2(sT - m_new)
        l_sc[...] = alpha * l_sc[...] + jnp.sum(pT, axis=0, keepdims=True)
        p_sc[slot][...] = pT.astype(BF16)
        a_sc[slot] = alpha
        m_sc[...] = m_new

    def accumulate(t, slot):
        pv = jnp.dot(vT_sc[key_block(t)], p_sc[slot][...], preferred_element_type=F32)
        acc_sc[...] = a_sc[slot] * acc_sc[...] + pv

    scores(0, 0, masked=True)

    @pl.when(n_blocks == 1)
    def _():
        softmax(0)
        accumulate(0, 0)

    @pl.when(n_blocks >= 2)
    def _():
        scores(1, 1)
        softmax(0)
        n_steady = n_blocks - 2

        def tick_pair(i, carry):
            t = 1 + 2 * i
            accumulate(t - 1, 0)
            scores(t + 1, 0)
            softmax(1)
            accumulate(t, 1)
            scores(t + 2, 1)
            softmax(0)
            return carry

        lax.fori_loop(0, n_steady // 2, tick_pair, 0)
        last = n_blocks - 1

        @pl.when(n_steady % 2 == 1)
        def _():
            accumulate(last - 2, 0)
            scores(last, 0)
            softmax(1)
            accumulate(last - 1, 1)
            softmax(0)
            accumulate(last, 0)

        @pl.when(n_steady % 2 == 0)
        def _():
            accumulate(last - 1, 0)
            softmax(1)
            accumulate(last, 1)

    lp = lam_ref[...]
    lam = (jnp.exp(jnp.sum(lp[0:1] * lp[1:2], axis=-1, keepdims=True))
           - jnp.exp(jnp.sum(lp[2:3] * lp[3:4], axis=-1, keepdims=True)) + lam_init)
    o_all = acc_sc[...] * (1.0 / l_sc[...])
    oT = o_all[:, 0:blk] - lam * o_all[:, blk:2 * blk]
    inv_rms = lax.rsqrt(jnp.mean(oT * oT, axis=0, keepdims=True) + SUBLN_EPS)
    oT = oT * inv_rms * (g_ref[...] * (1.0 - lam_init))
    o_ref[0] = oT.T.astype(o_ref.dtype)


def diff_attention(qkv, lam_params, subln_g, batch, layer_idx, *, blk=512):
    M, W3 = qkv.shape
    W = W3 // 3
    S = M // batch
    heads = W // (2 * HEAD_DIM)
    hw = 2 * HEAD_DIM
    lam_init = 0.8 - 0.6 * float(np.exp(-0.3 * layer_idx))
    view = qkv.reshape(batch, S, W3)
    out = pl.pallas_call(
        functools.partial(_diff_attn_kernel, blk=blk, lam_init=lam_init),
        grid=(batch, heads, S // blk),
        in_specs=[
            pl.BlockSpec((4, HEAD_DIM), lambda b, h, i: (0, 0)),
            pl.BlockSpec((1, blk, hw), lambda b, h, i: (b, i, h)),
            pl.BlockSpec((1, S, hw), lambda b, h, i: (b, 0, heads + h)),
            pl.BlockSpec((1, S, hw), lambda b, h, i: (b, 0, 2 * heads + h)),
            pl.BlockSpec((hw, 1), lambda b, h, i: (0, 0)),
        ],
        out_specs=pl.BlockSpec((1, blk, hw), lambda b, h, i: (b, i, h)),
        out_shape=jax.ShapeDtypeStruct((batch, S, W), BF16),
        scratch_shapes=[
            pltpu.VMEM((hw, 2 * blk), BF16),
            pltpu.VMEM((S // blk, hw, blk), BF16),
            pltpu.VMEM((blk, 2 * blk), F32), pltpu.VMEM((blk, 2 * blk), F32),
            pltpu.VMEM((blk, 2 * blk), BF16), pltpu.VMEM((blk, 2 * blk), BF16),
            pltpu.VMEM((2, 1, 2 * blk), F32),
            pltpu.VMEM((1, 2 * blk), F32), pltpu.VMEM((1, 2 * blk), F32),
            pltpu.VMEM((hw, 2 * blk), F32),
        ],
        compiler_params=_params("arbitrary", "arbitrary", "arbitrary"),
        name="diff_attn",
    )(lam_params, view, view, view, subln_g.reshape(hw, 1))
    return out.reshape(M, W)


def _ffn(x, h, w_in, w_out, widx, g_post, g_next):
    act = matmul_swiglu(h, w_in, widx, tm=1024, tn=512)
    y = matmul(act, w_out, widx, tm=1024, tn=512)
    return resid_norm(x, y, g_post, g_next, 0.5)


def kernel(x, positions, norm_g, ffn_in, ffn_out, dil_w_in, dil_w_out, diff_w_in, diff_lambda,
           diff_subln_g, diff_w_out):
    B, S, D = x.shape
    depth = norm_g.shape[0]
    M = B * S
    x = x.reshape(M, D)
    g = norm_g.reshape(depth * 6, 1, D)
    ffn_in, ffn_out, dil_w_in, dil_w_out, diff_w_in, diff_w_out = (
        w.astype(BF16) for w in (ffn_in, ffn_out, dil_w_in, dil_w_out, diff_w_in, diff_w_out))
    tabs = rope_tables(positions)

    h = rms_cast(x, g[0])
    for l in range(depth):
        gl = g[6 * l:6 * l + 6]
        x, h = _ffn(x, h, ffn_in, ffn_out, (l, 0), gl[1], gl[2])
        if l % 2 == 0:
            qkv = matmul_rope(h, dil_w_in, (l // 2,), tabs, tm=1024, tn=768)
            parts = [dilated_group_attention(qkv, grp, dil, B) for grp, (_, dil) in enumerate(DIL_PATTERNS)]
            mixed = dilated_mix([p[0] for p in parts], [p[1] for p in parts])
            y = matmul(mixed, dil_w_out, (l // 2,), tm=1024, tn=512)
        else:
            qkv = matmul_rope(h, diff_w_in, (l // 2,), tabs, tm=1024, tn=1024)
            att = diff_attention(qkv, diff_lambda[l // 2], diff_subln_g[l // 2], B, l)
            y = matmul(att, diff_w_out, (l // 2,), tm=1024, tn=512)
        x, h = resid_norm(x, y, gl[3], gl[4], 1.0)
        g_next = g[6 * l + 6] if l + 1 < depth else None
        x, h = _ffn(x, h, ffn_in, ffn_out, (l, 1), gl[5], g_next)
    return x.reshape(B, S, D)
```

```python
import functools

import numpy as np
import jax
import jax.numpy as jnp
from jax import lax
from jax.experimental import pallas as pl
from jax.experimental.pallas import tpu as pltpu

HEAD_DIM = 128
ROPE_THETA = 500000.0
ROPE_DIM = HEAD_DIM // 4
ROPE_HALF = ROPE_DIM // 2
NORM_EPS = 1e-6
SUBLN_EPS = 1e-5
DIL_PATTERNS = ((128, 1), (512, 4), (2048, 16))
DIL_GROUPS = len(DIL_PATTERNS)
DIL_HEADS_PER_GROUP = 10
DIL_HEADS = DIL_GROUPS * DIL_HEADS_PER_GROUP
DIL_WIDTH = DIL_HEADS * HEAD_DIM
DIL_GROUP_WIDTH = DIL_HEADS_PER_GROUP * HEAD_DIM
BAND_BLOCK = 128
LANES = 128
LN2 = float(np.log(2.0))
Q_PRESCALE = float(HEAD_DIM ** -0.5 / np.log(2.0))
assert all(window // dil == BAND_BLOCK for window, dil in DIL_PATTERNS)

VMEM_LIMIT_BYTES = 56 * 1024 * 1024

F32 = jnp.float32
BF16 = jnp.bfloat16


def _params(*sem):
    return pltpu.CompilerParams(dimension_semantics=sem, vmem_limit_bytes=VMEM_LIMIT_BYTES)


def _rms(x, eps):
    return x * lax.rsqrt(jnp.mean(x * x, axis=-1, keepdims=True) + eps)


def _rms_cast_kernel(x_ref, g_ref, o_ref):
    o_ref[...] = (_rms(x_ref[...], NORM_EPS) * g_ref[...]).astype(o_ref.dtype)


def rms_cast(x, g, *, tm=256):
    M, D = x.shape
    return pl.pallas_call(
        _rms_cast_kernel,
        grid=(M // tm,),
        in_specs=[pl.BlockSpec((tm, D), lambda i: (i, 0)), pl.BlockSpec((1, D), lambda i: (0, 0))],
        out_specs=pl.BlockSpec((tm, D), lambda i: (i, 0)),
        out_shape=jax.ShapeDtypeStruct((M, D), BF16),
        compiler_params=_params("parallel"),
        name="rms_cast",
    )(x, g)


def _resid_kernel(x_ref, y_ref, gp_ref, gn_ref, xo_ref, ho_ref, *, coef):
    xn = x_ref[...] + coef * (_rms(y_ref[...], NORM_EPS) * gp_ref[...])
    xo_ref[...] = xn
    ho_ref[...] = (_rms(xn, NORM_EPS) * gn_ref[...]).astype(ho_ref.dtype)


def _resid_last_kernel(x_ref, y_ref, gp_ref, xo_ref, *, coef):
    xo_ref[...] = x_ref[...] + coef * (_rms(y_ref[...], NORM_EPS) * gp_ref[...])


def resid_norm(x, y, g_post, g_next, coef, *, tm=256):
    M, D = x.shape
    row = pl.BlockSpec((tm, D), lambda i: (i, 0))
    vec = pl.BlockSpec((1, D), lambda i: (0, 0))
    if g_next is None:
        return pl.pallas_call(
            functools.partial(_resid_last_kernel, coef=coef),
            grid=(M // tm,),
            in_specs=[row, row, vec],
            out_specs=row,
            out_shape=jax.ShapeDtypeStruct((M, D), F32),
            compiler_params=_params("parallel"),
            name="resid_last",
        )(x, y, g_post), None
    return pl.pallas_call(
        functools.partial(_resid_kernel, coef=coef),
        grid=(M // tm,),
        in_specs=[row, row, vec, vec],
        out_specs=[row, row],
        out_shape=[jax.ShapeDtypeStruct((M, D), F32), jax.ShapeDtypeStruct((M, D), BF16)],
        compiler_params=_params("parallel"),
        name="resid_norm",
    )(x, y, g_post, g_next)


def _mm_kernel(a_ref, w_ref, o_ref):
    o_ref[...] = jnp.dot(a_ref[...], w_ref[...], preferred_element_type=F32).astype(o_ref.dtype)


def _weight_spec(w, widx, tn, col=lambda j: j):
    K = w.shape[-2]
    return pl.BlockSpec((None,) * len(widx) + (K, tn), lambda i, j: (*widx, 0, col(j)))


def matmul(a, w, widx, *, tm, tn, out_dtype=F32):
    M, K = a.shape
    N = w.shape[-1]
    return pl.pallas_call(
        _mm_kernel,
        grid=(M // tm, N // tn),
        in_specs=[pl.BlockSpec((tm, K), lambda i, j: (i, 0)), _weight_spec(w, widx, tn)],
        out_specs=pl.BlockSpec((tm, tn), lambda i, j: (i, j)),
        out_shape=jax.ShapeDtypeStruct((M, N), out_dtype),
        compiler_params=_params("parallel", "arbitrary"),
        name="matmul",
    )(a, w)


def _mm_swiglu_kernel(a_ref, wg_ref, wu_ref, o_ref):
    a = a_ref[...]
    gate = jnp.dot(a, wg_ref[...], preferred_element_type=F32)
    up = jnp.dot(a, wu_ref[...], preferred_element_type=F32)
    o_ref[...] = (gate * jax.nn.sigmoid(gate) * up).astype(o_ref.dtype)


def matmul_swiglu(a, w_in, widx, *, tm, tn):
    M, K = a.shape
    F = w_in.shape[-1] // 2
    nj = F // tn
    return pl.pallas_call(
        _mm_swiglu_kernel,
        grid=(M // tm, nj),
        in_specs=[
            pl.BlockSpec((tm, K), lambda i, j: (i, 0)),
            _weight_spec(w_in, widx, tn),
            _weight_spec(w_in, widx, tn, lambda j: j + nj),
        ],
        out_specs=pl.BlockSpec((tm, tn), lambda i, j: (i, j)),
        out_shape=jax.ShapeDtypeStruct((M, F), BF16),
        compiler_params=_params("parallel", "arbitrary"),
        name="matmul_swiglu",
    )(a, w_in, w_in)


def _rope_table_kernel(pos_ref, invf_ref, c_ref, s1_ref, s2_ref):
    ang = pos_ref[...].astype(F32) * invf_ref[...]
    lane = lax.broadcasted_iota(jnp.int32, ang.shape, 1)
    cos = jnp.cos(ang)
    sin = jnp.sin(ang)
    c_ref[...] = jnp.where(lane < ROPE_DIM, cos, 1.0)
    s1_ref[...] = jnp.where(lane < ROPE_HALF, -sin, 0.0)
    s2_ref[...] = jnp.where((lane >= ROPE_HALF) & (lane < ROPE_DIM), sin, 0.0)


def rope_tables(positions, *, tm=512):
    M = positions.size
    inv_freq = 1.0 / (ROPE_THETA ** (jnp.arange(0, ROPE_DIM, 2, dtype=F32) / ROPE_DIM))
    invf = jnp.concatenate([inv_freq, inv_freq, jnp.zeros((LANES - ROPE_DIM,), F32)]).reshape(1, LANES)
    tab = jax.ShapeDtypeStruct((M, LANES), F32)
    row = pl.BlockSpec((tm, LANES), lambda i: (i, 0))
    return pl.pallas_call(
        _rope_table_kernel,
        grid=(M // tm,),
        in_specs=[pl.BlockSpec((tm, 1), lambda i: (i, 0)), pl.BlockSpec((1, LANES), lambda i: (0, 0))],
        out_specs=[row, row, row],
        out_shape=[tab, tab, tab],
        compiler_params=_params("parallel"),
        name="rope_tables",
    )(positions.reshape(M, 1), invf)


ROPE_ROW_CHUNK = 256


def _mm_rope_kernel(a_ref, w_ref, c_ref, s1_ref, s2_ref, o_ref, *, tiles_per_part):
    part = pl.program_id(1) // tiles_per_part
    is_rope = part < 2
    mult = jnp.where(part == 0, Q_PRESCALE, 1.0)
    for r0 in range(0, a_ref.shape[0], ROPE_ROW_CHUNK):
        rows = slice(r0, r0 + ROPE_ROW_CHUNK)
        acc = jnp.dot(a_ref[rows, :], w_ref[...], preferred_element_type=F32)
        c = jnp.where(is_rope, c_ref[rows, :] * mult, 1.0)
        s1 = jnp.where(is_rope, s1_ref[rows, :] * mult, 0.0)
        s2 = jnp.where(is_rope, s2_ref[rows, :] * mult, 0.0)
        for h in range(acc.shape[1] // HEAD_DIM):
            sl = slice(h * HEAD_DIM, (h + 1) * HEAD_DIM)
            t = acc[:, sl]
            fwd = pltpu.roll(t, HEAD_DIM - ROPE_HALF, 1)
            bwd = pltpu.roll(t, ROPE_HALF, 1)
            o_ref[rows, sl] = (t * c + fwd * s1 + bwd * s2).astype(o_ref.dtype)


def matmul_rope(a, w, widx, tabs, *, tm, tn, part_width, col0=0, col_stride=1):
    M, K = a.shape
    tiles_per_part = part_width // tn
    tab = pl.BlockSpec((tm, LANES), lambda i, j: (i, 0))

    def col(j):
        return col0 + (j // tiles_per_part) * col_stride * tiles_per_part + j % tiles_per_part

    return pl.pallas_call(
        functools.partial(_mm_rope_kernel, tiles_per_part=tiles_per_part),
        grid=(M // tm, 3 * tiles_per_part),
        in_specs=[pl.BlockSpec((tm, K), lambda i, j: (i, 0)), _weight_spec(w, widx, tn, col), tab, tab, tab],
        out_specs=pl.BlockSpec((tm, tn), lambda i, j: (i, j)),
        out_shape=jax.ShapeDtypeStruct((M, 3 * part_width), BF16),
        compiler_params=_params("parallel", "arbitrary"),
        name="matmul_rope",
    )(a, w, *tabs)


_NT = (((1,), (1,)), ((), ()))


def _dil_attn_kernel(q_ref, kp_ref, kc_ref, vp_ref, vc_ref, o_ref, lse_ref):
    n = pl.program_id(2)
    qi = lax.broadcasted_iota(jnp.int32, (BAND_BLOCK, BAND_BLOCK), 0)
    kj = lax.broadcasted_iota(jnp.int32, (BAND_BLOCK, BAND_BLOCK), 1)
    mask_cur = kj <= qi
    mask_prev = (kj >= qi) & (n > 0)
    lane = lax.broadcasted_iota(jnp.int32, (BAND_BLOCK, LANES), 1)
    lse_tile = jnp.zeros((BAND_BLOCK, LANES), F32)
    for j in range(DIL_HEADS_PER_GROUP):
        sl = slice(j * HEAD_DIM, (j + 1) * HEAD_DIM)
        q = q_ref[0, :, sl]
        s_c = lax.dot_general(q, kc_ref[0, :, sl], _NT, preferred_element_type=F32)
        s_p = lax.dot_general(q, kp_ref[0, :, sl], _NT, preferred_element_type=F32)
        s_c = jnp.where(mask_cur, s_c, -jnp.inf)
        s_p = jnp.where(mask_prev, s_p, -jnp.inf)
        m = jnp.maximum(jnp.max(s_c, axis=-1, keepdims=True), jnp.max(s_p, axis=-1, keepdims=True))
        p_c = jnp.exp2(s_c - m)
        p_p = jnp.exp2(s_p - m)
        den = jnp.sum(p_c, axis=-1, keepdims=True) + jnp.sum(p_p, axis=-1, keepdims=True)
        o = (jnp.dot(p_c.astype(BF16), vc_ref[0, :, sl], preferred_element_type=F32)
             + jnp.dot(p_p.astype(BF16), vp_ref[0, :, sl], preferred_element_type=F32))
        o_ref[0, :, sl] = o / den
        lse_tile = jnp.where(lane == j, m * LN2 + jnp.log(den), lse_tile)
    lse_ref[0] = lse_tile


def dilated_group_attention(qkv, group, dilation, batch):
    M = qkv.shape[0]
    S = M // batch
    L = S // dilation
    nb = L // BAND_BLOCK
    view = qkv.reshape(batch, L, dilation * 3 * DIL_GROUP_WIDTH)
    blk = (1, BAND_BLOCK, DIL_GROUP_WIDTH)

    def spec(part, prev):
        def index(b, r, n):
            row = jnp.maximum(n - 1, 0) if prev else n
            return (b, row, r * 3 + part)
        return pl.BlockSpec(blk, index)

    o, lse = pl.pallas_call(
        _dil_attn_kernel,
        grid=(batch, dilation, nb),
        in_specs=[spec(0, False), spec(1, True), spec(1, False), spec(2, True), spec(2, False)],
        out_specs=[pl.BlockSpec(blk, lambda b, r, n: (b, n, r)),
                   pl.BlockSpec((1, BAND_BLOCK, LANES), lambda b, r, n: (b, n, r))],
        out_shape=[jax.ShapeDtypeStruct((batch, L, dilation * DIL_GROUP_WIDTH), F32),
                   jax.ShapeDtypeStruct((batch, L, dilation * LANES), F32)],
        compiler_params=_params("parallel", "parallel", "arbitrary"),
        name=f"dil_attn_g{group}",
    )(view, view, view, view, view)
    return o.reshape(M, DIL_GROUP_WIDTH), lse.reshape(M, LANES)


def _dil_mix_kernel(o0_ref, o1_ref, o2_ref, l0_ref, l1_ref, l2_ref, out_ref):
    lses = [l0_ref[...], l1_ref[...], l2_ref[...]]
    mx = jnp.maximum(jnp.maximum(lses[0], lses[1]), lses[2])
    es = [jnp.exp(l - mx) for l in lses]
    tot = es[0] + es[1] + es[2]
    for g, o_ref in enumerate((o0_ref, o1_ref, o2_ref)):
        alpha = es[g] / tot
        for j in range(DIL_HEADS_PER_GROUP):
            src = slice(j * HEAD_DIM, (j + 1) * HEAD_DIM)
            col = (g * DIL_HEADS_PER_GROUP + j) * HEAD_DIM
            out_ref[:, col:col + HEAD_DIM] = (alpha[:, j:j + 1] * o_ref[:, src]).astype(out_ref.dtype)


def dilated_mix(outs, lses, *, tm=256):
    M = outs[0].shape[0]
    o_spec = pl.BlockSpec((tm, DIL_GROUP_WIDTH), lambda i: (i, 0))
    l_spec = pl.BlockSpec((tm, LANES), lambda i: (i, 0))
    return pl.pallas_call(
        _dil_mix_kernel,
        grid=(M // tm,),
        in_specs=[o_spec] * 3 + [l_spec] * 3,
        out_specs=pl.BlockSpec((tm, DIL_WIDTH), lambda i: (i, 0)),
        out_shape=jax.ShapeDtypeStruct((M, DIL_WIDTH), BF16),
        compiler_params=_params("parallel"),
        name="dil_mix",
    )(*outs, *lses)


def _diff_attn_kernel(lam_ref, q_ref, k_ref, v_ref, g_ref, o_ref, qbd_sc, vT_sc, s0_sc, s1_sc, p0_sc, p1_sc,
                      a_sc, m_sc, l_sc, acc_sc, *, blk, lam_init):
    qi = pl.program_id(2)
    n_blocks = qi + 1
    s_sc = (s0_sc, s1_sc)
    p_sc = (p0_sc, p1_sc)

    @pl.when(qi == 0)
    def _():
        for kb in range(v_ref.shape[1] // blk):
            vT_sc[kb] = v_ref[0, kb * blk:(kb + 1) * blk, :].astype(F32).T.astype(BF16)
        qbd_sc[...] = jnp.zeros(qbd_sc.shape, BF16)

    qT = q_ref[0].astype(F32).T.astype(BF16)
    qbd_sc[0:HEAD_DIM, 0:blk] = qT[0:HEAD_DIM]
    qbd_sc[HEAD_DIM:2 * HEAD_DIM, blk:2 * blk] = qT[HEAD_DIM:2 * HEAD_DIM]
    m_sc[...] = jnp.full(m_sc.shape, -jnp.inf, F32)
    l_sc[...] = jnp.zeros(l_sc.shape, F32)
    acc_sc[...] = jnp.zeros(acc_sc.shape, F32)

    def key_block(t):
        return jnp.where(t == 0, qi, t - 1)

    def scores(t, slot, masked=False):
        k0 = pl.multiple_of(key_block(t) * blk, blk)
        sT = jnp.dot(k_ref[0, pl.ds(k0, blk), :], qbd_sc[...], preferred_element_type=F32)
        if masked:
            key = lax.broadcasted_iota(jnp.int32, sT.shape, 0)
            qry = lax.broadcasted_iota(jnp.int32, sT.shape, 1) & (blk - 1)
            sT = jnp.where(key <= qry, sT, -jnp.inf)
        s_sc[slot][...] = sT

    def softmax(slot):
        sT = s_sc[slot][...]
        m_prev = m_sc[...]
        m_new = jnp.maximum(m_prev, jnp.max(sT, axis=0, keepdims=True))
        alpha = jnp.exp2(m_prev - m_new)
        pT = jnp.exp2(sT - m_new)
        l_sc[...] = alpha * l_sc[...] + jnp.sum(pT, axis=0, keepdims=True)
        p_sc[slot][...] = pT.astype(BF16)
        a_sc[slot] = alpha
        m_sc[...] = m_new

    def accumulate(t, slot):
        pv = jnp.dot(vT_sc[key_block(t)], p_sc[slot][...], preferred_element_type=F32)
        acc_sc[...] = a_sc[slot] * acc_sc[...] + pv

    scores(0, 0, masked=True)

    @pl.when(n_blocks == 1)
    def _():
        softmax(0)
        accumulate(0, 0)

    @pl.when(n_blocks >= 2)
    def _():
        scores(1, 1)
        softmax(0)
        n_steady = n_blocks - 2

        def tick_pair(i, carry):
            t = 1 + 2 * i
            accumulate(t - 1, 0)
            scores(t + 1, 0)
            softmax(1)
            accumulate(t, 1)
            scores(t + 2, 1)
            softmax(0)
            return carry

        lax.fori_loop(0, n_steady // 2, tick_pair, 0)
        last = n_blocks - 1

        @pl.when(n_steady % 2 == 1)
        def _():
            accumulate(last - 2, 0)
            scores(last, 0)
            softmax(1)
            accumulate(last - 1, 1)
            softmax(0)
            accumulate(last, 0)

        @pl.when(n_steady % 2 == 0)
        def _():
            accumulate(last - 1, 0)
            softmax(1)
            accumulate(last, 1)

    lp = lam_ref[...]
    lam = (jnp.exp(jnp.sum(lp[0:1] * lp[1:2], axis=-1, keepdims=True))
           - jnp.exp(jnp.sum(lp[2:3] * lp[3:4], axis=-1, keepdims=True)) + lam_init)
    o_all = acc_sc[...] * (1.0 / l_sc[...])
    oT = o_all[:, 0:blk] - lam * o_all[:, blk:2 * blk]
    inv_rms = lax.rsqrt(jnp.mean(oT * oT, axis=0, keepdims=True) + SUBLN_EPS)
    oT = oT * inv_rms * (g_ref[...] * (1.0 - lam_init))
    o_ref[0] = oT.T.astype(o_ref.dtype)


def diff_attention(qkv, lam_params, subln_g, batch, layer_idx, *, blk=512):
    M, W3 = qkv.shape
    W = W3 // 3
    S = M // batch
    heads = W // (2 * HEAD_DIM)
    hw = 2 * HEAD_DIM
    lam_init = 0.8 - 0.6 * float(np.exp(-0.3 * layer_idx))
    view = qkv.reshape(batch, S, W3)
    out = pl.pallas_call(
        functools.partial(_diff_attn_kernel, blk=blk, lam_init=lam_init),
        grid=(batch, heads, S // blk),
        in_specs=[
            pl.BlockSpec((4, HEAD_DIM), lambda b, h, i: (0, 0)),
            pl.BlockSpec((1, blk, hw), lambda b, h, i: (b, i, h)),
            pl.BlockSpec((1, S, hw), lambda b, h, i: (b, 0, heads + h)),
            pl.BlockSpec((1, S, hw), lambda b, h, i: (b, 0, 2 * heads + h)),
            pl.BlockSpec((hw, 1), lambda b, h, i: (0, 0)),
        ],
        out_specs=pl.BlockSpec((1, blk, hw), lambda b, h, i: (b, i, h)),
        out_shape=jax.ShapeDtypeStruct((batch, S, W), BF16),
        scratch_shapes=[
            pltpu.VMEM((hw, 2 * blk), BF16),
            pltpu.VMEM((S // blk, hw, blk), BF16),
            pltpu.VMEM((blk, 2 * blk), F32), pltpu.VMEM((blk, 2 * blk), F32),
            pltpu.VMEM((blk, 2 * blk), BF16), pltpu.VMEM((blk, 2 * blk), BF16),
            pltpu.VMEM((2, 1, 2 * blk), F32),
            pltpu.VMEM((1, 2 * blk), F32), pltpu.VMEM((1, 2 * blk), F32),
            pltpu.VMEM((hw, 2 * blk), F32),
        ],
        compiler_params=_params("arbitrary", "arbitrary", "arbitrary"),
        name="diff_attn",
    )(lam_params, view, view, view, subln_g.reshape(hw, 1))
    return out.reshape(M, W)


def _ffn(x, h, w_in, w_out, widx, g_post, g_next):
    act = matmul_swiglu(h, w_in, widx, tm=1024, tn=512)
    y = matmul(act, w_out, widx, tm=1024, tn=512)
    return resid_norm(x, y, g_post, g_next, 0.5)


def kernel(x, positions, norm_g, ffn_in, ffn_out, dil_w_in, dil_w_out, diff_w_in, diff_lambda,
           diff_subln_g, diff_w_out):
    B, S, D = x.shape
    depth = norm_g.shape[0]
    M = B * S
    x = x.reshape(M, D)
    g = norm_g.reshape(depth * 6, 1, D)
    ffn_in, ffn_out, dil_w_in, dil_w_out, diff_w_in, diff_w_out = (
        w.astype(BF16) for w in (ffn_in, ffn_out, dil_w_in, dil_w_out, diff_w_in, diff_w_out))
    tabs = rope_tables(positions)

    h = rms_cast(x, g[0])
    for l in range(depth):
        gl = g[6 * l:6 * l + 6]
        x, h = _ffn(x, h, ffn_in, ffn_out, (l, 0), gl[1], gl[2])
        if l % 2 == 0:
            parts = []
            for grp, (_, dil) in enumerate(DIL_PATTERNS):
                qkv = matmul_rope(h, dil_w_in, (l // 2,), tabs, tm=1024, tn=DIL_GROUP_WIDTH,
                                  part_width=DIL_GROUP_WIDTH, col0=grp, col_stride=DIL_GROUPS)
                parts.append(dilated_group_attention(qkv, grp, dil, B))
            mixed = dilated_mix([p[0] for p in parts], [p[1] for p in parts])
            y = matmul(mixed, dil_w_out, (l // 2,), tm=1024, tn=512)
        else:
            qkv = matmul_rope(h, diff_w_in, (l // 2,), tabs, tm=1024, tn=1024,
                              part_width=diff_w_in.shape[-1] // 3)
            att = diff_attention(qkv, diff_lambda[l // 2], diff_subln_g[l // 2], B, l)
            y = matmul(att, diff_w_out, (l // 2,), tm=1024, tn=512)
        x, h = resid_norm(x, y, gl[3], gl[4], 1.0)
        g_next = g[6 * l + 6] if l + 1 < depth else None
        x, h = _ffn(x, h, ffn_in, ffn_out, (l, 1), gl[5], g_next)
    return x.reshape(B, S, D)
```

```python
import functools

import numpy as np
import jax
import jax.numpy as jnp
from jax import lax
from jax.experimental import pallas as pl
from jax.experimental.pallas import tpu as pltpu

HEAD_DIM = 128
ROPE_THETA = 500000.0
ROPE_DIM = HEAD_DIM // 4
ROPE_HALF = ROPE_DIM // 2
NORM_EPS = 1e-6
SUBLN_EPS = 1e-5
DIL_PATTERNS = ((128, 1), (512, 4), (2048, 16))
DIL_GROUPS = len(DIL_PATTERNS)
DIL_HEADS_PER_GROUP = 10
DIL_HEADS = DIL_GROUPS * DIL_HEADS_PER_GROUP
DIL_WIDTH = DIL_HEADS * HEAD_DIM
DIL_GROUP_WIDTH = DIL_HEADS_PER_GROUP * HEAD_DIM
BAND_BLOCK = 128
LANES = 128
LN2 = float(np.log(2.0))
Q_PRESCALE = float(HEAD_DIM ** -0.5 / np.log(2.0))
assert all(window // dil == BAND_BLOCK for window, dil in DIL_PATTERNS)

VMEM_LIMIT_BYTES = 56 * 1024 * 1024

F32 = jnp.float32
BF16 = jnp.bfloat16


def _params(*sem):
    return pltpu.CompilerParams(dimension_semantics=sem, vmem_limit_bytes=VMEM_LIMIT_BYTES)


def _rms(x, eps):
    return x * lax.rsqrt(jnp.mean(x * x, axis=-1, keepdims=True) + eps)


def _rms_cast_kernel(x_ref, g_ref, o_ref):
    o_ref[...] = (_rms(x_ref[...], NORM_EPS) * g_ref[...]).astype(o_ref.dtype)


def rms_cast(x, g, *, tm=256):
    M, D = x.shape
    return pl.pallas_call(
        _rms_cast_kernel,
        grid=(M // tm,),
        in_specs=[pl.BlockSpec((tm, D), lambda i: (i, 0)), pl.BlockSpec((1, D), lambda i: (0, 0))],
        out_specs=pl.BlockSpec((tm, D), lambda i: (i, 0)),
        out_shape=jax.ShapeDtypeStruct((M, D), BF16),
        compiler_params=_params("parallel"),
        name="rms_cast",
    )(x, g)


def _weight_spec(w, widx, tn, col=lambda j: j):
    K = w.shape[-2]
    return pl.BlockSpec((None,) * len(widx) + (K, tn), lambda i, j: (*widx, 0, col(j)))


RESID_COL_CHUNK = 1024
RESID_ROW_CHUNK = 64


def _mm_resid_kernel(a_ref, w_ref, x_ref, gp_ref, gn_ref, xo_ref, *maybe_ho_ref, coef):
    k = pl.program_id(1)

    @pl.when(k == 0)
    def _():
        xo_ref[...] = jnp.zeros(xo_ref.shape, F32)

    a = a_ref[...]
    for n0 in range(0, xo_ref.shape[1], RESID_COL_CHUNK):
        cols = slice(n0, n0 + RESID_COL_CHUNK)
        xo_ref[:, cols] += jnp.dot(a, w_ref[:, cols], preferred_element_type=F32)

    @pl.when(k == pl.num_programs(1) - 1)
    def _():
        gain = coef * gp_ref[...]
        for r0 in range(0, xo_ref.shape[0], RESID_ROW_CHUNK):
            rows = slice(r0, r0 + RESID_ROW_CHUNK)
            xn = x_ref[rows, :] + _rms(xo_ref[rows, :], NORM_EPS) * gain
            xo_ref[rows, :] = xn
            for ho_ref in maybe_ho_ref:
                ho_ref[rows, :] = (_rms(xn, NORM_EPS) * gn_ref[...]).astype(ho_ref.dtype)


def matmul_resid(a, w, widx, x, g_post, g_next, coef, *, tm, tk):
    M, K = a.shape
    D = w.shape[-1]
    row = pl.BlockSpec((tm, D), lambda i, k: (i, 0))
    vec = pl.BlockSpec((1, D), lambda i, k: (0, 0))
    with_h = g_next is not None
    out = pl.pallas_call(
        functools.partial(_mm_resid_kernel, coef=coef),
        grid=(M // tm, K // tk),
        in_specs=[pl.BlockSpec((tm, tk), lambda i, k: (i, k)),
                  pl.BlockSpec((None,) * len(widx) + (tk, D), lambda i, k: (*widx, k, 0)),
                  row, vec, vec],
        out_specs=[row, row] if with_h else [row],
        out_shape=[jax.ShapeDtypeStruct((M, D), F32)] + ([jax.ShapeDtypeStruct((M, D), BF16)] if with_h else []),
        compiler_params=_params("parallel", "arbitrary"),
        name="matmul_resid",
    )(a, w, x, g_post, g_next if with_h else g_post)
    return (out[0], out[1]) if with_h else (out[0], None)


def _mm_swiglu_kernel(a_ref, wg_ref, wu_ref, o_ref):
    a = a_ref[...]
    gate = jnp.dot(a, wg_ref[...], preferred_element_type=F32)
    up = jnp.dot(a, wu_ref[...], preferred_element_type=F32)
    o_ref[...] = (gate * jax.nn.sigmoid(gate) * up).astype(o_ref.dtype)


def matmul_swiglu(a, w_in, widx, *, tm, tn):
    M, K = a.shape
    F = w_in.shape[-1] // 2
    nj = F // tn
    return pl.pallas_call(
        _mm_swiglu_kernel,
        grid=(M // tm, nj),
        in_specs=[
            pl.BlockSpec((tm, K), lambda i, j: (i, 0)),
            _weight_spec(w_in, widx, tn),
            _weight_spec(w_in, widx, tn, lambda j: j + nj),
        ],
        out_specs=pl.BlockSpec((tm, tn), lambda i, j: (i, j)),
        out_shape=jax.ShapeDtypeStruct((M, F), BF16),
        compiler_params=_params("parallel", "arbitrary"),
        name="matmul_swiglu",
    )(a, w_in, w_in)


def _rope_table_kernel(pos_ref, invf_ref, c_ref, s1_ref, s2_ref):
    ang = pos_ref[...].astype(F32) * invf_ref[...]
    lane = lax.broadcasted_iota(jnp.int32, ang.shape, 1)
    cos = jnp.cos(ang)
    sin = jnp.sin(ang)
    c_ref[...] = jnp.where(lane < ROPE_DIM, cos, 1.0)
    s1_ref[...] = jnp.where(lane < ROPE_HALF, -sin, 0.0)
    s2_ref[...] = jnp.where((lane >= ROPE_HALF) & (lane < ROPE_DIM), sin, 0.0)


def rope_tables(positions, *, tm=512):
    M = positions.size
    inv_freq = 1.0 / (ROPE_THETA ** (jnp.arange(0, ROPE_DIM, 2, dtype=F32) / ROPE_DIM))
    invf = jnp.concatenate([inv_freq, inv_freq, jnp.zeros((LANES - ROPE_DIM,), F32)]).reshape(1, LANES)
    tab = jax.ShapeDtypeStruct((M, LANES), F32)
    row = pl.BlockSpec((tm, LANES), lambda i: (i, 0))
    return pl.pallas_call(
        _rope_table_kernel,
        grid=(M // tm,),
        in_specs=[pl.BlockSpec((tm, 1), lambda i: (i, 0)), pl.BlockSpec((1, LANES), lambda i: (0, 0))],
        out_specs=[row, row, row],
        out_shape=[tab, tab, tab],
        compiler_params=_params("parallel"),
        name="rope_tables",
    )(positions.reshape(M, 1), invf)


ROPE_ROW_CHUNK = 256


def _mm_rope_kernel(a_ref, w_ref, c_ref, s1_ref, s2_ref, o_ref, *, tiles_per_part):
    part = pl.program_id(1) // tiles_per_part
    is_rope = part < 2
    mult = jnp.where(part == 0, Q_PRESCALE, 1.0)
    for r0 in range(0, a_ref.shape[0], ROPE_ROW_CHUNK):
        rows = slice(r0, r0 + ROPE_ROW_CHUNK)
        acc = jnp.dot(a_ref[rows, :], w_ref[...], preferred_element_type=F32)
        c = jnp.where(is_rope, c_ref[rows, :] * mult, 1.0)
        s1 = jnp.where(is_rope, s1_ref[rows, :] * mult, 0.0)
        s2 = jnp.where(is_rope, s2_ref[rows, :] * mult, 0.0)
        for h in range(acc.shape[1] // HEAD_DIM):
            sl = slice(h * HEAD_DIM, (h + 1) * HEAD_DIM)
            t = acc[:, sl]
            fwd = pltpu.roll(t, HEAD_DIM - ROPE_HALF, 1)
            bwd = pltpu.roll(t, ROPE_HALF, 1)
            o_ref[rows, sl] = (t * c + fwd * s1 + bwd * s2).astype(o_ref.dtype)


def matmul_rope(a, w, widx, tabs, *, tm, tn, part_width, col0=0, col_stride=1):
    M, K = a.shape
    tiles_per_part = part_width // tn
    tab = pl.BlockSpec((tm, LANES), lambda i, j: (i, 0))

    def col(j):
        return col0 + (j // tiles_per_part) * col_stride * tiles_per_part + j % tiles_per_part

    return pl.pallas_call(
        functools.partial(_mm_rope_kernel, tiles_per_part=tiles_per_part),
        grid=(M // tm, 3 * tiles_per_part),
        in_specs=[pl.BlockSpec((tm, K), lambda i, j: (i, 0)), _weight_spec(w, widx, tn, col), tab, tab, tab],
        out_specs=pl.BlockSpec((tm, tn), lambda i, j: (i, j)),
        out_shape=jax.ShapeDtypeStruct((M, 3 * part_width), BF16),
        compiler_params=_params("parallel", "arbitrary"),
        name="matmul_rope",
    )(a, w, *tabs)


_NT = (((1,), (1,)), ((), ()))


def _dil_attn_kernel(q_ref, kp_ref, kc_ref, vp_ref, vc_ref, o_ref, lse_ref):
    n = pl.program_id(2)
    qi = lax.broadcasted_iota(jnp.int32, (BAND_BLOCK, 2 * BAND_BLOCK), 0)
    kj = lax.broadcasted_iota(jnp.int32, (BAND_BLOCK, 2 * BAND_BLOCK), 1)
    first_key = jnp.where(n > 0, qi, jnp.maximum(qi, BAND_BLOCK))
    mask = (kj >= first_key) & (kj <= qi + BAND_BLOCK)
    lane = lax.broadcasted_iota(jnp.int32, (BAND_BLOCK, LANES), 1)
    lse_tile = jnp.zeros((BAND_BLOCK, LANES), F32)
    for j in range(DIL_HEADS_PER_GROUP):
        sl = slice(j * HEAD_DIM, (j + 1) * HEAD_DIM)
        k = jnp.concatenate([kp_ref[0, :, sl], kc_ref[0, :, sl]], axis=0)
        v = jnp.concatenate([vp_ref[0, :, sl], vc_ref[0, :, sl]], axis=0)
        s = lax.dot_general(q_ref[0, :, sl], k, _NT, preferred_element_type=F32)
        s = jnp.where(mask, s, -jnp.inf)
        m = jnp.max(s, axis=-1, keepdims=True)
        p = jnp.exp2(s - m)
        den = jnp.sum(p, axis=-1, keepdims=True)
        o = jnp.dot(p.astype(BF16), v, preferred_element_type=F32)
        o_ref[0, :, sl] = o * (1.0 / den)
        lse_tile = jnp.where(lane == j, m * LN2 + jnp.log(den), lse_tile)
    lse_ref[0] = lse_tile


def dilated_group_attention(qkv, group, dilation, batch):
    M = qkv.shape[0]
    S = M // batch
    L = S // dilation
    nb = L // BAND_BLOCK
    view = qkv.reshape(batch, L, dilation * 3 * DIL_GROUP_WIDTH)
    blk = (1, BAND_BLOCK, DIL_GROUP_WIDTH)

    def spec(part, prev):
        def index(b, r, n):
            row = jnp.maximum(n - 1, 0) if prev else n
            return (b, row, r * 3 + part)
        return pl.BlockSpec(blk, index)

    o, lse = pl.pallas_call(
        _dil_attn_kernel,
        grid=(batch, dilation, nb),
        in_specs=[spec(0, False), spec(1, True), spec(1, False), spec(2, True), spec(2, False)],
        out_specs=[pl.BlockSpec(blk, lambda b, r, n: (b, n, r)),
                   pl.BlockSpec((1, BAND_BLOCK, LANES), lambda b, r, n: (b, n, r))],
        out_shape=[jax.ShapeDtypeStruct((batch, L, dilation * DIL_GROUP_WIDTH), F32),
                   jax.ShapeDtypeStruct((batch, L, dilation * LANES), F32)],
        compiler_params=_params("parallel", "parallel", "arbitrary"),
        name=f"dil_attn_g{group}",
    )(view, view, view, view, view)
    return o.reshape(M, DIL_GROUP_WIDTH), lse.reshape(M, LANES)


def _dil_mix_kernel(o0_ref, o1_ref, o2_ref, l0_ref, l1_ref, l2_ref, out_ref):
    lses = [l0_ref[...], l1_ref[...], l2_ref[...]]
    mx = jnp.maximum(jnp.maximum(lses[0], lses[1]), lses[2])
    es = [jnp.exp(l - mx) for l in lses]
    tot = es[0] + es[1] + es[2]
    for g, o_ref in enumerate((o0_ref, o1_ref, o2_ref)):
        alpha = es[g] / tot
        for j in range(DIL_HEADS_PER_GROUP):
            src = slice(j * HEAD_DIM, (j + 1) * HEAD_DIM)
            col = (g * DIL_HEADS_PER_GROUP + j) * HEAD_DIM
            out_ref[:, col:col + HEAD_DIM] = (alpha[:, j:j + 1] * o_ref[:, src]).astype(out_ref.dtype)


def dilated_mix(outs, lses, *, tm=256):
    M = outs[0].shape[0]
    o_spec = pl.BlockSpec((tm, DIL_GROUP_WIDTH), lambda i: (i, 0))
    l_spec = pl.BlockSpec((tm, LANES), lambda i: (i, 0))
    return pl.pallas_call(
        _dil_mix_kernel,
        grid=(M // tm,),
        in_specs=[o_spec] * 3 + [l_spec] * 3,
        out_specs=pl.BlockSpec((tm, DIL_WIDTH), lambda i: (i, 0)),
        out_shape=jax.ShapeDtypeStruct((M, DIL_WIDTH), BF16),
        compiler_params=_params("parallel"),
        name="dil_mix",
    )(*outs, *lses)


def _diff_attn_kernel(lam_ref, q_ref, k_ref, v_ref, g_ref, o_ref, qbd_sc, vT_sc, s0_sc, s1_sc, p0_sc, p1_sc,
                      bm0_sc, bm1_sc, a0_sc, a1_sc, m_sc, l_sc, acc_sc, *, blk, lam_init):
    qi = pl.program_id(2)
    n_blocks = qi + 1
    s_sc = (s0_sc, s1_sc)
    p_sc = (p0_sc, p1_sc)
    bm_sc = (bm0_sc, bm1_sc)
    a_sc = (a0_sc, a1_sc)

    @pl.when(qi == 0)
    def _():
        for kb in range(v_ref.shape[1] // blk):
            vT_sc[kb] = v_ref[0, kb * blk:(kb + 1) * blk, :].astype(F32).T.astype(BF16)
        qbd_sc[...] = jnp.zeros(qbd_sc.shape, BF16)

    qT = q_ref[0].astype(F32).T.astype(BF16)
    qbd_sc[0:HEAD_DIM, 0:blk] = qT[0:HEAD_DIM]
    qbd_sc[HEAD_DIM:2 * HEAD_DIM, blk:2 * blk] = qT[HEAD_DIM:2 * HEAD_DIM]
    m_sc[...] = jnp.full(m_sc.shape, -jnp.inf, F32)
    l_sc[...] = jnp.zeros(l_sc.shape, F32)
    acc_sc[...] = jnp.zeros(acc_sc.shape, F32)

    def key_block(t):
        return jnp.where(t == 0, qi, t - 1)

    def scores(t, slot, masked=False):
        k0 = pl.multiple_of(key_block(t) * blk, blk)
        sT = jnp.dot(k_ref[0, pl.ds(k0, blk), :], qbd_sc[...], preferred_element_type=F32)
        if masked:
            key = lax.broadcasted_iota(jnp.int32, sT.shape, 0)
            qry = lax.broadcasted_iota(jnp.int32, sT.shape, 1) & (blk - 1)
            sT = jnp.where(key <= qry, sT, -jnp.inf)
        s_sc[slot][...] = sT
        bm_sc[slot][...] = jnp.max(sT, axis=0, keepdims=True)

    def softmax(slot):
        sT = s_sc[slot][...]
        m_prev = m_sc[...]
        m_new = jnp.maximum(m_prev, bm_sc[slot][...])
        alpha = jnp.exp2(m_prev - m_new)
        pT = jnp.exp2(sT - m_new)
        l_sc[...] = alpha * l_sc[...] + jnp.sum(pT, axis=0, keepdims=True)
        p_sc[slot][...] = pT.astype(BF16)
        a_sc[slot][...] = alpha
        m_sc[...] = m_new

    def accumulate(t, slot):
        pv = jnp.dot(vT_sc[key_block(t)], p_sc[slot][...], preferred_element_type=F32)
        acc_sc[...] = a_sc[slot][...] * acc_sc[...] + pv

    scores(0, 0, masked=True)

    @pl.when(n_blocks == 1)
    def _():
        softmax(0)
        accumulate(0, 0)

    @pl.when(n_blocks >= 2)
    def _():
        scores(1, 1)
        softmax(0)
        n_steady = n_blocks - 2

        def tick_pair(i, carry):
            t = 1 + 2 * i
            softmax(1)
            accumulate(t - 1, 0)
            scores(t + 1, 0)
            softmax(0)
            accumulate(t, 1)
            scores(t + 2, 1)
            return carry

        lax.fori_loop(0, n_steady // 2, tick_pair, 0)
        last = n_blocks - 1

        @pl.when(n_steady % 2 == 1)
        def _():
            accumulate(last - 2, 0)
            scores(last, 0)
            softmax(1)
            accumulate(last - 1, 1)
            softmax(0)
            accumulate(last, 0)

        @pl.when(n_steady % 2 == 0)
        def _():
            accumulate(last - 1, 0)
            softmax(1)
            accumulate(last, 1)

    lp = lam_ref[...]
    lam = (jnp.exp(jnp.sum(lp[0:1] * lp[1:2], axis=-1, keepdims=True))
           - jnp.exp(jnp.sum(lp[2:3] * lp[3:4], axis=-1, keepdims=True)) + lam_init)
    o_all = acc_sc[...] * (1.0 / l_sc[...])
    oT = o_all[:, 0:blk] - lam * o_all[:, blk:2 * blk]
    inv_rms = lax.rsqrt(jnp.mean(oT * oT, axis=0, keepdims=True) + SUBLN_EPS)
    oT = oT * inv_rms * (g_ref[...] * (1.0 - lam_init))
    o_ref[0] = oT.T.astype(o_ref.dtype)


def diff_attention(qkv, lam_params, subln_g, batch, layer_idx, *, blk=512):
    M, W3 = qkv.shape
    W = W3 // 3
    S = M // batch
    heads = W // (2 * HEAD_DIM)
    hw = 2 * HEAD_DIM
    lam_init = 0.8 - 0.6 * float(np.exp(-0.3 * layer_idx))
    view = qkv.reshape(batch, S, W3)
    out = pl.pallas_call(
        functools.partial(_diff_attn_kernel, blk=blk, lam_init=lam_init),
        grid=(batch, heads, S // blk),
        in_specs=[
            pl.BlockSpec((4, HEAD_DIM), lambda b, h, i: (0, 0)),
            pl.BlockSpec((1, blk, hw), lambda b, h, i: (b, i, h)),
            pl.BlockSpec((1, S, hw), lambda b, h, i: (b, 0, heads + h)),
            pl.BlockSpec((1, S, hw), lambda b, h, i: (b, 0, 2 * heads + h)),
            pl.BlockSpec((hw, 1), lambda b, h, i: (0, 0)),
        ],
        out_specs=pl.BlockSpec((1, blk, hw), lambda b, h, i: (b, i, h)),
        out_shape=jax.ShapeDtypeStruct((batch, S, W), BF16),
        scratch_shapes=[
            pltpu.VMEM((hw, 2 * blk), BF16),
            pltpu.VMEM((S // blk, hw, blk), BF16),
            pltpu.VMEM((blk, 2 * blk), F32), pltpu.VMEM((blk, 2 * blk), F32),
            pltpu.VMEM((blk, 2 * blk), BF16), pltpu.VMEM((blk, 2 * blk), BF16),
            pltpu.VMEM((1, 2 * blk), F32), pltpu.VMEM((1, 2 * blk), F32),
            pltpu.VMEM((1, 2 * blk), F32), pltpu.VMEM((1, 2 * blk), F32),
            pltpu.VMEM((1, 2 * blk), F32), pltpu.VMEM((1, 2 * blk), F32),
            pltpu.VMEM((hw, 2 * blk), F32),
        ],
        compiler_params=_params("arbitrary", "arbitrary", "arbitrary"),
        name="diff_attn",
    )(lam_params, view, view, view, subln_g.reshape(hw, 1))
    return out.reshape(M, W)


def _ffn(x, h, w_in, w_out, widx, g_post, g_next):
    act = matmul_swiglu(h, w_in, widx, tm=1024, tn=512)
    return matmul_resid(act, w_out, widx, x, g_post, g_next, 0.5, tm=512, tk=512)


def kernel(x, positions, norm_g, ffn_in, ffn_out, dil_w_in, dil_w_out, diff_w_in, diff_lambda,
           diff_subln_g, diff_w_out):
    B, S, D = x.shape
    depth = norm_g.shape[0]
    M = B * S
    x = x.reshape(M, D)
    g = norm_g.reshape(depth * 6, 1, D)
    ffn_in, ffn_out, dil_w_in, dil_w_out, diff_w_in, diff_w_out = (
        w.astype(BF16) for w in (ffn_in, ffn_out, dil_w_in, dil_w_out, diff_w_in, diff_w_out))
    tabs = rope_tables(positions)

    h = rms_cast(x, g[0])
    for l in range(depth):
        gl = g[6 * l:6 * l + 6]
        x, h = _ffn(x, h, ffn_in, ffn_out, (l, 0), gl[1], gl[2])
        if l % 2 == 0:
            parts = []
            for grp, (_, dil) in enumerate(DIL_PATTERNS):
                qkv = matmul_rope(h, dil_w_in, (l // 2,), tabs, tm=1024, tn=DIL_GROUP_WIDTH,
                                  part_width=DIL_GROUP_WIDTH, col0=grp, col_stride=DIL_GROUPS)
                parts.append(dilated_group_attention(qkv, grp, dil, B))
            mixed = dilated_mix([p[0] for p in parts], [p[1] for p in parts])
            x, h = matmul_resid(mixed, dil_w_out, (l // 2,), x, gl[3], gl[4], 1.0, tm=512, tk=768)
        else:
            qkv = matmul_rope(h, diff_w_in, (l // 2,), tabs, tm=1024, tn=1024,
                              part_width=diff_w_in.shape[-1] // 3)
            att = diff_attention(qkv, diff_lambda[l // 2], diff_subln_g[l // 2], B, l)
            x, h = matmul_resid(att, diff_w_out, (l // 2,), x, gl[3], gl[4], 1.0, tm=512, tk=512)
        g_next = g[6 * l + 6] if l + 1 < depth else None
        x, h = _ffn(x, h, ffn_in, ffn_out, (l, 1), gl[5], g_next)
    return x.reshape(B, S, D)
```

```python
import functools

import numpy as np
import jax
import jax.numpy as jnp
from jax import lax
from jax.experimental import pallas as pl
from jax.experimental.pallas import tpu as pltpu

HEAD_DIM = 128
ROPE_THETA = 500000.0
ROPE_DIM = HEAD_DIM // 4
ROPE_HALF = ROPE_DIM // 2
NORM_EPS = 1e-6
SUBLN_EPS = 1e-5
DIL_PATTERNS = ((128, 1), (512, 4), (2048, 16))
DIL_GROUPS = len(DIL_PATTERNS)
DIL_HEADS_PER_GROUP = 10
DIL_HEADS = DIL_GROUPS * DIL_HEADS_PER_GROUP
DIL_WIDTH = DIL_HEADS * HEAD_DIM
DIL_GROUP_WIDTH = DIL_HEADS_PER_GROUP * HEAD_DIM
BAND_BLOCK = 128
LANES = 128
LN2 = float(np.log(2.0))
Q_PRESCALE = float(HEAD_DIM ** -0.5 / np.log(2.0))
assert all(window // dil == BAND_BLOCK for window, dil in DIL_PATTERNS)

VMEM_LIMIT_BYTES = 56 * 1024 * 1024

F32 = jnp.float32
BF16 = jnp.bfloat16


def _params(*sem):
    return pltpu.CompilerParams(dimension_semantics=sem, vmem_limit_bytes=VMEM_LIMIT_BYTES)


def _rms(x, eps):
    return x * lax.rsqrt(jnp.mean(x * x, axis=-1, keepdims=True) + eps)


def _rms_cast_kernel(x_ref, g_ref, o_ref):
    o_ref[...] = (_rms(x_ref[...], NORM_EPS) * g_ref[...]).astype(o_ref.dtype)


def rms_cast(x, g, *, tm=256):
    M, D = x.shape
    return pl.pallas_call(
        _rms_cast_kernel,
        grid=(M // tm,),
        in_specs=[pl.BlockSpec((tm, D), lambda i: (i, 0)), pl.BlockSpec((1, D), lambda i: (0, 0))],
        out_specs=pl.BlockSpec((tm, D), lambda i: (i, 0)),
        out_shape=jax.ShapeDtypeStruct((M, D), BF16),
        compiler_params=_params("parallel"),
        name="rms_cast",
    )(x, g)


def _weight_spec(w, widx, tn, col=lambda j: j):
    K = w.shape[-2]
    return pl.BlockSpec((None,) * len(widx) + (K, tn), lambda i, j: (*widx, 0, col(j)))


RESID_COL_CHUNK = 1024
RESID_ROW_CHUNK = 128


def _mm_resid_kernel(a_ref, w_ref, x_ref, gp_ref, gn_ref, xo_ref, *maybe_ho_ref, coef):
    k = pl.program_id(1)

    def partial_products():
        a = a_ref[...]
        for n0 in range(0, xo_ref.shape[1], RESID_COL_CHUNK):
            cols = slice(n0, n0 + RESID_COL_CHUNK)
            yield cols, jnp.dot(a, w_ref[:, cols], preferred_element_type=F32)

    @pl.when(k == 0)
    def _():
        for cols, part in partial_products():
            xo_ref[:, cols] = part

    @pl.when(k > 0)
    def _():
        for cols, part in partial_products():
            xo_ref[:, cols] += part

    @pl.when(k == pl.num_programs(1) - 1)
    def _():
        gain = coef * gp_ref[...]

        for r0 in range(0, xo_ref.shape[0], RESID_ROW_CHUNK):
            rows = slice(r0, r0 + RESID_ROW_CHUNK)
            xn = x_ref[rows, :] + _rms(xo_ref[rows, :], NORM_EPS) * gain
            xo_ref[rows, :] = xn
            for ho_ref in maybe_ho_ref:
                ho_ref[rows, :] = (_rms(xn, NORM_EPS) * gn_ref[...]).astype(ho_ref.dtype)


def matmul_resid(a, w, widx, x, g_post, g_next, coef, *, tm, tk):
    M, K = a.shape
    D = w.shape[-1]
    row = pl.BlockSpec((tm, D), lambda i, k: (i, 0))
    vec = pl.BlockSpec((1, D), lambda i, k: (0, 0))
    with_h = g_next is not None
    out = pl.pallas_call(
        functools.partial(_mm_resid_kernel, coef=coef),
        grid=(M // tm, K // tk),
        in_specs=[pl.BlockSpec((tm, tk), lambda i, k: (i, k)),
                  pl.BlockSpec((None,) * len(widx) + (tk, D), lambda i, k: (*widx, k, 0)),
                  row, vec, vec],
        out_specs=[row, row] if with_h else [row],
        out_shape=[jax.ShapeDtypeStruct((M, D), F32)] + ([jax.ShapeDtypeStruct((M, D), BF16)] if with_h else []),
        compiler_params=_params("parallel", "arbitrary"),
        name="matmul_resid",
    )(a, w, x, g_post, g_next if with_h else g_post)
    return (out[0], out[1]) if with_h else (out[0], None)


def _mm_swiglu_kernel(a_ref, wg_ref, wu_ref, o_ref):
    a = a_ref[...]
    gate = jnp.dot(a, wg_ref[...], preferred_element_type=F32)
    up = jnp.dot(a, wu_ref[...], preferred_element_type=F32)
    o_ref[...] = (gate * jax.nn.sigmoid(gate) * up).astype(o_ref.dtype)


def matmul_swiglu(a, w_in, widx, *, tm, tn):
    M, K = a.shape
    F = w_in.shape[-1] // 2
    nj = F // tn
    return pl.pallas_call(
        _mm_swiglu_kernel,
        grid=(M // tm, nj),
        in_specs=[
            pl.BlockSpec((tm, K), lambda i, j: (i, 0)),
            _weight_spec(w_in, widx, tn),
            _weight_spec(w_in, widx, tn, lambda j: j + nj),
        ],
        out_specs=pl.BlockSpec((tm, tn), lambda i, j: (i, j)),
        out_shape=jax.ShapeDtypeStruct((M, F), BF16),
        compiler_params=_params("parallel", "arbitrary"),
        name="matmul_swiglu",
    )(a, w_in, w_in)


def _rope_table_kernel(pos_ref, invf_ref, c_ref, s1_ref, s2_ref):
    ang = pos_ref[...].astype(F32) * invf_ref[...]
    lane = lax.broadcasted_iota(jnp.int32, ang.shape, 1)
    cos = jnp.cos(ang)
    sin = jnp.sin(ang)
    c_ref[...] = jnp.where(lane < ROPE_DIM, cos, 1.0)
    s1_ref[...] = jnp.where(lane < ROPE_HALF, -sin, 0.0)
    s2_ref[...] = jnp.where((lane >= ROPE_HALF) & (lane < ROPE_DIM), sin, 0.0)


def rope_tables(positions, *, tm=512):
    M = positions.size
    inv_freq = 1.0 / (ROPE_THETA ** (jnp.arange(0, ROPE_DIM, 2, dtype=F32) / ROPE_DIM))
    invf = jnp.concatenate([inv_freq, inv_freq, jnp.zeros((LANES - ROPE_DIM,), F32)]).reshape(1, LANES)
    tab = jax.ShapeDtypeStruct((M, LANES), F32)
    row = pl.BlockSpec((tm, LANES), lambda i: (i, 0))
    return pl.pallas_call(
        _rope_table_kernel,
        grid=(M // tm,),
        in_specs=[pl.BlockSpec((tm, 1), lambda i: (i, 0)), pl.BlockSpec((1, LANES), lambda i: (0, 0))],
        out_specs=[row, row, row],
        out_shape=[tab, tab, tab],
        compiler_params=_params("parallel"),
        name="rope_tables",
    )(positions.reshape(M, 1), invf)


ROPE_ROW_CHUNK = 256


def _mm_rope_kernel(a_ref, w_ref, c_ref, s1_ref, s2_ref, o_ref, *, tiles_per_part):
    part = pl.program_id(1) // tiles_per_part
    is_rope = part < 2
    mult = jnp.where(part == 0, Q_PRESCALE, 1.0)
    for r0 in range(0, a_ref.shape[0], ROPE_ROW_CHUNK):
        rows = slice(r0, r0 + ROPE_ROW_CHUNK)
        acc = jnp.dot(a_ref[rows, :], w_ref[...], preferred_element_type=F32)
        c = jnp.where(is_rope, c_ref[rows, :] * mult, 1.0)
        s1 = jnp.where(is_rope, s1_ref[rows, :] * mult, 0.0)
        s2 = jnp.where(is_rope, s2_ref[rows, :] * mult, 0.0)
        for h in range(acc.shape[1] // HEAD_DIM):
            sl = slice(h * HEAD_DIM, (h + 1) * HEAD_DIM)
            t = acc[:, sl]
            fwd = pltpu.roll(t, HEAD_DIM - ROPE_HALF, 1)
            bwd = pltpu.roll(t, ROPE_HALF, 1)
            o_ref[rows, sl] = (t * c + fwd * s1 + bwd * s2).astype(o_ref.dtype)


def matmul_rope(a, w, widx, tabs, *, tm, tn, part_width, col0=0, col_stride=1):
    M, K = a.shape
    tiles_per_part = part_width // tn
    tab = pl.BlockSpec((tm, LANES), lambda i, j: (i, 0))

    def col(j):
        return col0 + (j // tiles_per_part) * col_stride * tiles_per_part + j % tiles_per_part

    return pl.pallas_call(
        functools.partial(_mm_rope_kernel, tiles_per_part=tiles_per_part),
        grid=(M // tm, 3 * tiles_per_part),
        in_specs=[pl.BlockSpec((tm, K), lambda i, j: (i, 0)), _weight_spec(w, widx, tn, col), tab, tab, tab],
        out_specs=pl.BlockSpec((tm, tn), lambda i, j: (i, j)),
        out_shape=jax.ShapeDtypeStruct((M, 3 * part_width), BF16),
        compiler_params=_params("parallel", "arbitrary"),
        name="matmul_rope",
    )(a, w, *tabs)


_NT = (((1,), (1,)), ((), ()))


def _dil_attn_kernel(q_ref, kp_ref, kc_ref, vp_ref, vc_ref, o_ref, lse_ref):
    n = pl.program_id(2)
    qi = lax.broadcasted_iota(jnp.int32, (BAND_BLOCK, 2 * BAND_BLOCK), 0)
    kj = lax.broadcasted_iota(jnp.int32, (BAND_BLOCK, 2 * BAND_BLOCK), 1)
    first_key = jnp.where(n > 0, qi, jnp.maximum(qi, BAND_BLOCK))
    mask = (kj >= first_key) & (kj <= qi + BAND_BLOCK)
    lane = lax.broadcasted_iota(jnp.int32, (BAND_BLOCK, LANES), 1)
    lse_tile = jnp.zeros((BAND_BLOCK, LANES), F32)
    for j in range(DIL_HEADS_PER_GROUP):
        sl = slice(j * HEAD_DIM, (j + 1) * HEAD_DIM)
        k = jnp.concatenate([kp_ref[0, :, sl], kc_ref[0, :, sl]], axis=0)
        v = jnp.concatenate([vp_ref[0, :, sl], vc_ref[0, :, sl]], axis=0)
        s = lax.dot_general(q_ref[0, :, sl], k, _NT, preferred_element_type=F32)
        s = jnp.where(mask, s, -jnp.inf)
        m = jnp.max(s, axis=-1, keepdims=True)
        p = jnp.exp2(s - m)
        den = jnp.sum(p, axis=-1, keepdims=True)
        o = jnp.dot(p.astype(BF16), v, preferred_element_type=F32)
        o_ref[0, :, sl] = o * (1.0 / den)
        lse_tile = jnp.where(lane == j, m * LN2 + jnp.log(den), lse_tile)
    lse_ref[0] = lse_tile


def dilated_group_attention(qkv, group, dilation, batch):
    M = qkv.shape[0]
    S = M // batch
    L = S // dilation
    nb = L // BAND_BLOCK
    view = qkv.reshape(batch, L, dilation * 3 * DIL_GROUP_WIDTH)
    blk = (1, BAND_BLOCK, DIL_GROUP_WIDTH)

    def spec(part, prev):
        def index(b, r, n):
            row = jnp.maximum(n - 1, 0) if prev else n
            return (b, row, r * 3 + part)
        return pl.BlockSpec(blk, index)

    o, lse = pl.pallas_call(
        _dil_attn_kernel,
        grid=(batch, dilation, nb),
        in_specs=[spec(0, False), spec(1, True), spec(1, False), spec(2, True), spec(2, False)],
        out_specs=[pl.BlockSpec(blk, lambda b, r, n: (b, n, r)),
                   pl.BlockSpec((1, BAND_BLOCK, LANES), lambda b, r, n: (b, n, r))],
        out_shape=[jax.ShapeDtypeStruct((batch, L, dilation * DIL_GROUP_WIDTH), F32),
                   jax.ShapeDtypeStruct((batch, L, dilation * LANES), F32)],
        compiler_params=_params("parallel", "parallel", "arbitrary"),
        name=f"dil_attn_g{group}",
    )(view, view, view, view, view)
    return o.reshape(M, DIL_GROUP_WIDTH), lse.reshape(M, LANES)


def _dil_mix_kernel(o0_ref, o1_ref, o2_ref, l0_ref, l1_ref, l2_ref, out_ref):
    lses = [l0_ref[...], l1_ref[...], l2_ref[...]]
    mx = jnp.maximum(jnp.maximum(lses[0], lses[1]), lses[2])
    es = [jnp.exp(l - mx) for l in lses]
    tot = es[0] + es[1] + es[2]
    for g, o_ref in enumerate((o0_ref, o1_ref, o2_ref)):
        alpha = es[g] / tot
        for j in range(DIL_HEADS_PER_GROUP):
            src = slice(j * HEAD_DIM, (j + 1) * HEAD_DIM)
            col = (g * DIL_HEADS_PER_GROUP + j) * HEAD_DIM
            out_ref[:, col:col + HEAD_DIM] = (alpha[:, j:j + 1] * o_ref[:, src]).astype(out_ref.dtype)


def dilated_mix(outs, lses, *, tm=256):
    M = outs[0].shape[0]
    o_spec = pl.BlockSpec((tm, DIL_GROUP_WIDTH), lambda i: (i, 0))
    l_spec = pl.BlockSpec((tm, LANES), lambda i: (i, 0))
    return pl.pallas_call(
        _dil_mix_kernel,
        grid=(M // tm,),
        in_specs=[o_spec] * 3 + [l_spec] * 3,
        out_specs=pl.BlockSpec((tm, DIL_WIDTH), lambda i: (i, 0)),
        out_shape=jax.ShapeDtypeStruct((M, DIL_WIDTH), BF16),
        compiler_params=_params("parallel"),
        name="dil_mix",
    )(*outs, *lses)


ONES_ROWS = 16


def _diff_attn_kernel(lam_ref, q_ref, k_ref, v_ref, g_ref, o_ref, qbd_sc, vT_sc, s0_sc, s1_sc, p0_sc, p1_sc,
                      bm0_sc, bm1_sc, a0_sc, a1_sc, m_sc, acc_sc, *, blk, lam_init):
    s_sc = (s0_sc, s1_sc)
    p_sc = (p0_sc, p1_sc)
    bm_sc = (bm0_sc, bm1_sc)
    a_sc = (a0_sc, a1_sc)
    n_q = v_ref.shape[1] // blk

    hw = v_ref.shape[2]
    for kb in range(n_q):
        vT_sc[kb, 0:hw, :] = v_ref[0, kb * blk:(kb + 1) * blk, :].astype(F32).T.astype(BF16)
        vT_sc[kb, hw:hw + ONES_ROWS, :] = jnp.ones((ONES_ROWS, blk), BF16)
    qbd_sc[...] = jnp.zeros(qbd_sc.shape, BF16)

    def query_block(qi, carry):
        _diff_attn_query_block(qi, lam_ref, q_ref, k_ref, g_ref, o_ref, qbd_sc, vT_sc, s_sc, p_sc, bm_sc, a_sc,
                               m_sc, acc_sc, blk=blk, lam_init=lam_init)
        return carry

    lax.fori_loop(0, n_q, query_block, 0)


def _diff_attn_query_block(qi, lam_ref, q_ref, k_ref, g_ref, o_ref, qbd_sc, vT_sc, s_sc, p_sc, bm_sc, a_sc,
                           m_sc, acc_sc, *, blk, lam_init):
    n_blocks = qi + 1
    q_rows = pl.ds(pl.multiple_of(qi * blk, blk), blk)

    qT = q_ref[0, q_rows, :].astype(F32).T.astype(BF16)
    qbd_sc[0:HEAD_DIM, 0:blk] = qT[0:HEAD_DIM]
    qbd_sc[HEAD_DIM:2 * HEAD_DIM, blk:2 * blk] = qT[HEAD_DIM:2 * HEAD_DIM]
    m_sc[...] = jnp.full(m_sc.shape, -jnp.inf, F32)
    acc_sc[...] = jnp.zeros(acc_sc.shape, F32)

    def key_block(t):
        return jnp.where(t == 0, qi, t - 1)

    def scores(t, slot, masked=False):
        k0 = pl.multiple_of(key_block(t) * blk, blk)
        sT = jnp.dot(k_ref[0, pl.ds(k0, blk), :], qbd_sc[...], preferred_element_type=F32)
        if masked:
            key = lax.broadcasted_iota(jnp.int32, sT.shape, 0)
            qry = lax.broadcasted_iota(jnp.int32, sT.shape, 1) & (blk - 1)
            sT = jnp.where(key <= qry, sT, -jnp.inf)
        s_sc[slot][...] = sT
        bm_sc[slot][...] = jnp.max(sT, axis=0, keepdims=True)

    def softmax(slot):
        sT = s_sc[slot][...]
        m_prev = m_sc[...]
        m_new = jnp.maximum(m_prev, bm_sc[slot][...])
        alpha = jnp.exp2(m_prev - m_new)
        pT = jnp.exp2(sT - m_new)
        p_sc[slot][...] = pT.astype(BF16)
        a_sc[slot][...] = alpha
        m_sc[...] = m_new

    def accumulate(t, slot):
        pv = jnp.dot(vT_sc[key_block(t)], p_sc[slot][...], preferred_element_type=F32)
        acc_sc[...] = a_sc[slot][...] * acc_sc[...] + pv

    scores(0, 0, masked=True)

    @pl.when(n_blocks == 1)
    def _():
        softmax(0)
        accumulate(0, 0)

    @pl.when(n_blocks >= 2)
    def _():
        scores(1, 1)
        softmax(0)
        n_steady = n_blocks - 2

        def tick_pair(i, carry):
            t = 1 + 2 * i
            softmax(1)
            accumulate(t - 1, 0)
            scores(t + 1, 0)
            softmax(0)
            accumulate(t, 1)
            scores(t + 2, 1)
            return carry

        lax.fori_loop(0, n_steady // 2, tick_pair, 0)
        last = n_blocks - 1

        @pl.when(n_steady % 2 == 1)
        def _():
            accumulate(last - 2, 0)
            scores(last, 0)
            softmax(1)
            accumulate(last - 1, 1)
            softmax(0)
            accumulate(last, 0)

        @pl.when(n_steady % 2 == 0)
        def _():
            accumulate(last - 1, 0)
            softmax(1)
            accumulate(last, 1)

    lp = lam_ref[...]
    lam = (jnp.exp(jnp.sum(lp[0:1] * lp[1:2], axis=-1, keepdims=True))
           - jnp.exp(jnp.sum(lp[2:3] * lp[3:4], axis=-1, keepdims=True)) + lam_init)
    hw = 2 * HEAD_DIM
    o_all = acc_sc[0:hw, :] * (1.0 / acc_sc[hw:hw + 1, :])
    oT = o_all[:, 0:blk] - lam * o_all[:, blk:2 * blk]
    inv_rms = lax.rsqrt(jnp.mean(oT * oT, axis=0, keepdims=True) + SUBLN_EPS)
    oT = oT * inv_rms * (g_ref[...] * (1.0 - lam_init))
    o_ref[0, q_rows, :] = oT.T.astype(o_ref.dtype)


def diff_attention(qkv, lam_params, subln_g, batch, layer_idx, *, blk=512):
    M, W3 = qkv.shape
    W = W3 // 3
    S = M // batch
    heads = W // (2 * HEAD_DIM)
    hw = 2 * HEAD_DIM
    lam_init = 0.8 - 0.6 * float(np.exp(-0.3 * layer_idx))
    view = qkv.reshape(batch, S, W3)
    out = pl.pallas_call(
        functools.partial(_diff_attn_kernel, blk=blk, lam_init=lam_init),
        grid=(batch, heads),
        in_specs=[
            pl.BlockSpec((4, HEAD_DIM), lambda b, h: (0, 0)),
            pl.BlockSpec((1, S, hw), lambda b, h: (b, 0, h)),
            pl.BlockSpec((1, S, hw), lambda b, h: (b, 0, heads + h)),
            pl.BlockSpec((1, S, hw), lambda b, h: (b, 0, 2 * heads + h)),
            pl.BlockSpec((hw, 1), lambda b, h: (0, 0)),
        ],
        out_specs=pl.BlockSpec((1, S, hw), lambda b, h: (b, 0, h)),
        out_shape=jax.ShapeDtypeStruct((batch, S, W), BF16),
        scratch_shapes=[
            pltpu.VMEM((hw, 2 * blk), BF16),
            pltpu.VMEM((S // blk, hw + ONES_ROWS, blk), BF16),
            pltpu.VMEM((blk, 2 * blk), F32), pltpu.VMEM((blk, 2 * blk), F32),
            pltpu.VMEM((blk, 2 * blk), BF16), pltpu.VMEM((blk, 2 * blk), BF16),
            pltpu.VMEM((1, 2 * blk), F32), pltpu.VMEM((1, 2 * blk), F32),
            pltpu.VMEM((1, 2 * blk), F32), pltpu.VMEM((1, 2 * blk), F32),
            pltpu.VMEM((1, 2 * blk), F32),
            pltpu.VMEM((hw + ONES_ROWS, 2 * blk), F32),
        ],
        compiler_params=_params("parallel", "parallel"),
        name="diff_attn",
    )(lam_params, view, view, view, subln_g.reshape(hw, 1))
    return out.reshape(M, W)


def _ffn(x, h, w_in, w_out, widx, g_post, g_next):
    act = matmul_swiglu(h, w_in, widx, tm=1024, tn=512)
    return matmul_resid(act, w_out, widx, x, g_post, g_next, 0.5, tm=512, tk=512)


def kernel(x, positions, norm_g, ffn_in, ffn_out, dil_w_in, dil_w_out, diff_w_in, diff_lambda,
           diff_subln_g, diff_w_out):
    B, S, D = x.shape
    depth = norm_g.shape[0]
    M = B * S
    x = x.reshape(M, D)
    g = norm_g.reshape(depth * 6, 1, D)
    ffn_in, ffn_out, dil_w_in, dil_w_out, diff_w_in, diff_w_out = (
        w.astype(BF16) for w in (ffn_in, ffn_out, dil_w_in, dil_w_out, diff_w_in, diff_w_out))
    tabs = rope_tables(positions)

    h = rms_cast(x, g[0])
    for l in range(depth):
        gl = g[6 * l:6 * l + 6]
        x, h = _ffn(x, h, ffn_in, ffn_out, (l, 0), gl[1], gl[2])
        if l % 2 == 0:
            parts = []
            for grp, (_, dil) in enumerate(DIL_PATTERNS):
                qkv = matmul_rope(h, dil_w_in, (l // 2,), tabs, tm=1024, tn=DIL_GROUP_WIDTH,
                                  part_width=DIL_GROUP_WIDTH, col0=grp, col_stride=DIL_GROUPS)
                parts.append(dilated_group_attention(qkv, grp, dil, B))
            mixed = dilated_mix([p[0] for p in parts], [p[1] for p in parts])
            x, h = matmul_resid(mixed, dil_w_out, (l // 2,), x, gl[3], gl[4], 1.0, tm=512, tk=768)
        else:
            qkv = matmul_rope(h, diff_w_in, (l // 2,), tabs, tm=1024, tn=1024,
                              part_width=diff_w_in.shape[-1] // 3)
            att = diff_attention(qkv, diff_lambda[l // 2], diff_subln_g[l // 2], B, l)
            x, h = matmul_resid(att, diff_w_out, (l // 2,), x, gl[3], gl[4], 1.0, tm=512, tk=512)
        g_next = g[6 * l + 6] if l + 1 < depth else None
        x, h = _ffn(x, h, ffn_in, ffn_out, (l, 1), gl[5], g_next)
    return x.reshape(B, S, D)
```

```python
import functools

import numpy as np
import jax
import jax.numpy as jnp
from jax import lax
from jax.experimental import pallas as pl
from jax.experimental.pallas import tpu as pltpu

HEAD_DIM = 128
ROPE_THETA = 500000.0
ROPE_DIM = HEAD_DIM // 4
ROPE_HALF = ROPE_DIM // 2
NORM_EPS = 1e-6
SUBLN_EPS = 1e-5
DIL_PATTERNS = ((128, 1), (512, 4), (2048, 16))
DIL_GROUPS = len(DIL_PATTERNS)
DIL_HEADS_PER_GROUP = 10
DIL_HEADS = DIL_GROUPS * DIL_HEADS_PER_GROUP
DIL_WIDTH = DIL_HEADS * HEAD_DIM
DIL_GROUP_WIDTH = DIL_HEADS_PER_GROUP * HEAD_DIM
BAND_BLOCK = 128
LANES = 128
LN2 = float(np.log(2.0))
Q_PRESCALE = float(HEAD_DIM ** -0.5 / np.log(2.0))
assert all(window // dil == BAND_BLOCK for window, dil in DIL_PATTERNS)

VMEM_LIMIT_BYTES = 56 * 1024 * 1024

F32 = jnp.float32
BF16 = jnp.bfloat16


def _params(*sem):
    return pltpu.CompilerParams(dimension_semantics=sem, vmem_limit_bytes=VMEM_LIMIT_BYTES)


def _rms(x, eps):
    return x * lax.rsqrt(jnp.mean(x * x, axis=-1, keepdims=True) + eps)


def _rms_cast_kernel(x_ref, g_ref, o_ref):
    o_ref[...] = (_rms(x_ref[...], NORM_EPS) * g_ref[...]).astype(o_ref.dtype)


def rms_cast(x, g, *, tm=256):
    M, D = x.shape
    return pl.pallas_call(
        _rms_cast_kernel,
        grid=(M // tm,),
        in_specs=[pl.BlockSpec((tm, D), lambda i: (i, 0)), pl.BlockSpec((1, D), lambda i: (0, 0))],
        out_specs=pl.BlockSpec((tm, D), lambda i: (i, 0)),
        out_shape=jax.ShapeDtypeStruct((M, D), BF16),
        compiler_params=_params("parallel"),
        name="rms_cast",
    )(x, g)


def _weight_spec(w, widx, tn, col=lambda j: j):
    K = w.shape[-2]
    return pl.BlockSpec((None,) * len(widx) + (K, tn), lambda i, j: (*widx, 0, col(j)))


RESID_COL_CHUNK = 1024
RESID_ROW_CHUNK = 128


def _mm_resid_kernel(a_ref, w_ref, x_ref, gp_ref, gn_ref, xo_ref, *maybe_ho_ref, coef):
    k = pl.program_id(1)

    def partial_products():
        a = a_ref[...]
        for n0 in range(0, xo_ref.shape[1], RESID_COL_CHUNK):
            cols = slice(n0, n0 + RESID_COL_CHUNK)
            yield cols, jnp.dot(a, w_ref[:, cols], preferred_element_type=F32)

    @pl.when(k == 0)
    def _():
        for cols, part in partial_products():
            xo_ref[:, cols] = part

    @pl.when(k > 0)
    def _():
        for cols, part in partial_products():
            xo_ref[:, cols] += part

    @pl.when(k == pl.num_programs(1) - 1)
    def _():
        gain = coef * gp_ref[...]

        for r0 in range(0, xo_ref.shape[0], RESID_ROW_CHUNK):
            rows = slice(r0, r0 + RESID_ROW_CHUNK)
            xn = x_ref[rows, :] + _rms(xo_ref[rows, :], NORM_EPS) * gain
            xo_ref[rows, :] = xn
            for ho_ref in maybe_ho_ref:
                ho_ref[rows, :] = (_rms(xn, NORM_EPS) * gn_ref[...]).astype(ho_ref.dtype)


def matmul_resid(a, w, widx, x, g_post, g_next, coef, *, tm, tk):
    M, K = a.shape
    D = w.shape[-1]
    row = pl.BlockSpec((tm, D), lambda i, k: (i, 0))
    vec = pl.BlockSpec((1, D), lambda i, k: (0, 0))
    with_h = g_next is not None
    out = pl.pallas_call(
        functools.partial(_mm_resid_kernel, coef=coef),
        grid=(M // tm, K // tk),
        in_specs=[pl.BlockSpec((tm, tk), lambda i, k: (i, k)),
                  pl.BlockSpec((None,) * len(widx) + (tk, D), lambda i, k: (*widx, k, 0)),
                  row, vec, vec],
        out_specs=[row, row] if with_h else [row],
        out_shape=[jax.ShapeDtypeStruct((M, D), F32)] + ([jax.ShapeDtypeStruct((M, D), BF16)] if with_h else []),
        compiler_params=_params("parallel", "arbitrary"),
        name="matmul_resid",
    )(a, w, x, g_post, g_next if with_h else g_post)
    return (out[0], out[1]) if with_h else (out[0], None)


def _mm_swiglu_kernel(a_ref, wg_ref, wu_ref, o_ref):
    a = a_ref[...]
    gate = jnp.dot(a, wg_ref[...], preferred_element_type=F32)
    up = jnp.dot(a, wu_ref[...], preferred_element_type=F32)
    o_ref[...] = (gate * jax.nn.sigmoid(gate) * up).astype(o_ref.dtype)


def matmul_swiglu(a, w_in, widx, *, tm, tn):
    M, K = a.shape
    F = w_in.shape[-1] // 2
    nj = F // tn
    return pl.pallas_call(
        _mm_swiglu_kernel,
        grid=(M // tm, nj),
        in_specs=[
            pl.BlockSpec((tm, K), lambda i, j: (i, 0)),
            _weight_spec(w_in, widx, tn),
            _weight_spec(w_in, widx, tn, lambda j: j + nj),
        ],
        out_specs=pl.BlockSpec((tm, tn), lambda i, j: (i, j)),
        out_shape=jax.ShapeDtypeStruct((M, F), BF16),
        compiler_params=_params("parallel", "arbitrary"),
        name="matmul_swiglu",
    )(a, w_in, w_in)


def _rope_table_kernel(pos_ref, invf_ref, c_ref, s1_ref, s2_ref):
    ang = pos_ref[...].astype(F32) * invf_ref[...]
    lane = lax.broadcasted_iota(jnp.int32, ang.shape, 1)
    cos = jnp.cos(ang)
    sin = jnp.sin(ang)
    c_ref[...] = jnp.where(lane < ROPE_DIM, cos, 1.0)
    s1_ref[...] = jnp.where(lane < ROPE_HALF, -sin, 0.0)
    s2_ref[...] = jnp.where((lane >= ROPE_HALF) & (lane < ROPE_DIM), sin, 0.0)


def rope_tables(positions, *, tm=512):
    M = positions.size
    inv_freq = 1.0 / (ROPE_THETA ** (jnp.arange(0, ROPE_DIM, 2, dtype=F32) / ROPE_DIM))
    invf = jnp.concatenate([inv_freq, inv_freq, jnp.zeros((LANES - ROPE_DIM,), F32)]).reshape(1, LANES)
    tab = jax.ShapeDtypeStruct((M, LANES), F32)
    row = pl.BlockSpec((tm, LANES), lambda i: (i, 0))
    return pl.pallas_call(
        _rope_table_kernel,
        grid=(M // tm,),
        in_specs=[pl.BlockSpec((tm, 1), lambda i: (i, 0)), pl.BlockSpec((1, LANES), lambda i: (0, 0))],
        out_specs=[row, row, row],
        out_shape=[tab, tab, tab],
        compiler_params=_params("parallel"),
        name="rope_tables",
    )(positions.reshape(M, 1), invf)


ROPE_ROW_CHUNK = 256


def _mm_rope_kernel(a_ref, w_ref, c_ref, s1_ref, s2_ref, o_ref, *stage_refs, tiles_per_part, dilation):
    part = pl.program_id(1) // tiles_per_part
    is_rope = part < 2
    mult = jnp.where(part == 0, Q_PRESCALE, 1.0)
    for r0 in range(0, a_ref.shape[0], ROPE_ROW_CHUNK):
        rows = slice(r0, r0 + ROPE_ROW_CHUNK)
        acc = jnp.dot(a_ref[rows, :], w_ref[...], preferred_element_type=F32)
        c = jnp.where(is_rope, c_ref[rows, :] * mult, 1.0)
        s1 = jnp.where(is_rope, s1_ref[rows, :] * mult, 0.0)
        s2 = jnp.where(is_rope, s2_ref[rows, :] * mult, 0.0)
        stage = stage_refs[(r0 // ROPE_ROW_CHUNK) % 2] if dilation > 1 else None
        for h in range(acc.shape[1] // HEAD_DIM):
            sl = slice(h * HEAD_DIM, (h + 1) * HEAD_DIM)
            t = acc[:, sl]
            fwd = pltpu.roll(t, HEAD_DIM - ROPE_HALF, 1)
            bwd = pltpu.roll(t, ROPE_HALF, 1)
            roped = t * c + fwd * s1 + bwd * s2
            if dilation == 1:
                o_ref[rows, sl] = roped.astype(o_ref.dtype)
            else:
                stage[h] = roped
        if dilation > 1:
            n = ROPE_ROW_CHUNK // dilation
            sup, i0 = divmod(r0 // dilation, o_ref.shape[2])
            for h in range(acc.shape[1] // HEAD_DIM):
                sl = slice(h * HEAD_DIM, (h + 1) * HEAD_DIM)
                for r in range(dilation):
                    o_ref[sup, r, i0:i0 + n, sl] = stage[h, pl.ds(r, n, stride=dilation), :].astype(o_ref.dtype)


def matmul_rope(a, w, widx, tabs, *, tm, tn, part_width, col0=0, col_stride=1, dilation=1):
    M, K = a.shape
    tiles_per_part = part_width // tn
    tab = pl.BlockSpec((tm, LANES), lambda i, j: (i, 0))

    def col(j):
        return col0 + (j // tiles_per_part) * col_stride * tiles_per_part + j % tiles_per_part

    if dilation == 1:
        out_spec = pl.BlockSpec((tm, tn), lambda i, j: (i, j))
        out_shape = jax.ShapeDtypeStruct((M, 3 * part_width), BF16)
        scratch = []
    else:
        span = BAND_BLOCK * dilation
        tiles_per_span = max(span // tm, 1)
        out_spec = pl.BlockSpec((max(tm // span, 1), dilation, min(BAND_BLOCK, tm // dilation), tn),
                                lambda i, j: (i // tiles_per_span, 0, i % tiles_per_span, j))
        out_shape = jax.ShapeDtypeStruct((M // span, dilation, BAND_BLOCK, 3 * part_width), BF16)
        scratch = [pltpu.VMEM((tn // HEAD_DIM, ROPE_ROW_CHUNK, HEAD_DIM), F32)] * 2
    return pl.pallas_call(
        functools.partial(_mm_rope_kernel, tiles_per_part=tiles_per_part, dilation=dilation),
        grid=(M // tm, 3 * tiles_per_part),
        in_specs=[pl.BlockSpec((tm, K), lambda i, j: (i, 0)), _weight_spec(w, widx, tn, col), tab, tab, tab],
        out_specs=out_spec,
        out_shape=out_shape,
        scratch_shapes=scratch,
        compiler_params=_params("parallel", "arbitrary"),
        name="matmul_rope",
    )(a, w, *tabs)


_NT = (((1,), (1,)), ((), ()))


def _dil_attn_kernel(q_ref, kp_ref, kc_ref, vp_ref, vc_ref, o_ref, lse_ref):
    n = pl.program_id(2)
    qi = lax.broadcasted_iota(jnp.int32, (BAND_BLOCK, 2 * BAND_BLOCK), 0)
    kj = lax.broadcasted_iota(jnp.int32, (BAND_BLOCK, 2 * BAND_BLOCK), 1)
    first_key = jnp.where(n > 0, qi, jnp.maximum(qi, BAND_BLOCK))
    mask = (kj >= first_key) & (kj <= qi + BAND_BLOCK)
    lane = lax.broadcasted_iota(jnp.int32, (BAND_BLOCK, LANES), 1)
    lse_tile = jnp.zeros((BAND_BLOCK, LANES), F32)
    for j in range(DIL_HEADS_PER_GROUP):
        sl = slice(j * HEAD_DIM, (j + 1) * HEAD_DIM)
        k = jnp.concatenate([kp_ref[:, sl], kc_ref[:, sl]], axis=0)
        v = jnp.concatenate([vp_ref[:, sl], vc_ref[:, sl]], axis=0)
        s = lax.dot_general(q_ref[:, sl], k, _NT, preferred_element_type=F32)
        s = jnp.where(mask, s, -jnp.inf)
        m = jnp.max(s, axis=-1, keepdims=True)
        p = jnp.exp2(s - m)
        den = jnp.sum(p, axis=-1, keepdims=True)
        o = jnp.dot(p.astype(BF16), v, preferred_element_type=F32)
        o_ref[0, :, sl] = o * (1.0 / den)
        lse_tile = jnp.where(lane == j, m * LN2 + jnp.log(den), lse_tile)
    lse_ref[0] = lse_tile


def dilated_group_attention(qkv, group, dilation, batch):
    M = qkv.shape[0] * dilation * BAND_BLOCK
    S = M // batch
    L = S // dilation
    nb = L // BAND_BLOCK
    blk = (1, BAND_BLOCK, DIL_GROUP_WIDTH)

    def spec(part, prev):
        def index(b, r, n):
            return (b * nb + (jnp.maximum(n - 1, 0) if prev else n), r, 0, part)
        return pl.BlockSpec((None, None, BAND_BLOCK, DIL_GROUP_WIDTH), index)

    o, lse = pl.pallas_call(
        _dil_attn_kernel,
        grid=(batch, dilation, nb),
        in_specs=[spec(0, False), spec(1, True), spec(1, False), spec(2, True), spec(2, False)],
        out_specs=[pl.BlockSpec(blk, lambda b, r, n: (b, n, r)),
                   pl.BlockSpec((1, BAND_BLOCK, LANES), lambda b, r, n: (b, n, r))],
        out_shape=[jax.ShapeDtypeStruct((batch, L, dilation * DIL_GROUP_WIDTH), F32),
                   jax.ShapeDtypeStruct((batch, L, dilation * LANES), F32)],
        compiler_params=_params("parallel", "parallel", "arbitrary"),
        name=f"dil_attn_g{group}",
    )(qkv, qkv, qkv, qkv, qkv)
    return o.reshape(M, DIL_GROUP_WIDTH), lse.reshape(M, LANES)


def _dil_mix_kernel(o0_ref, o1_ref, o2_ref, l0_ref, l1_ref, l2_ref, out_ref):
    lses = [l0_ref[...], l1_ref[...], l2_ref[...]]
    mx = jnp.maximum(jnp.maximum(lses[0], lses[1]), lses[2])
    es = [jnp.exp(l - mx) for l in lses]
    tot = es[0] + es[1] + es[2]
    for g, o_ref in enumerate((o0_ref, o1_ref, o2_ref)):
        alpha = es[g] / tot
        for j in range(DIL_HEADS_PER_GROUP):
            src = slice(j * HEAD_DIM, (j + 1) * HEAD_DIM)
            col = (g * DIL_HEADS_PER_GROUP + j) * HEAD_DIM
            out_ref[:, col:col + HEAD_DIM] = (alpha[:, j:j + 1] * o_ref[:, src]).astype(out_ref.dtype)


def dilated_mix(outs, lses, *, tm=256):
    M = outs[0].shape[0]
    o_spec = pl.BlockSpec((tm, DIL_GROUP_WIDTH), lambda i: (i, 0))
    l_spec = pl.BlockSpec((tm, LANES), lambda i: (i, 0))
    return pl.pallas_call(
        _dil_mix_kernel,
        grid=(M // tm,),
        in_specs=[o_spec] * 3 + [l_spec] * 3,
        out_specs=pl.BlockSpec((tm, DIL_WIDTH), lambda i: (i, 0)),
        out_shape=jax.ShapeDtypeStruct((M, DIL_WIDTH), BF16),
        compiler_params=_params("parallel"),
        name="dil_mix",
    )(*outs, *lses)


ONES_ROWS = 16


def _diff_attn_kernel(lam_ref, q_ref, k_ref, v_ref, g_ref, o_ref, qbd_sc, vT_sc, s0_sc, s1_sc, p0_sc, p1_sc,
                      bm0_sc, bm1_sc, a0_sc, a1_sc, m_sc, acc_sc, *, blk, lam_init):
    s_sc = (s0_sc, s1_sc)
    p_sc = (p0_sc, p1_sc)
    bm_sc = (bm0_sc, bm1_sc)
    a_sc = (a0_sc, a1_sc)
    n_q = v_ref.shape[1] // blk

    hw = v_ref.shape[2]
    for kb in range(n_q):
        vT_sc[kb, 0:hw, :] = v_ref[0, kb * blk:(kb + 1) * blk, :].astype(F32).T.astype(BF16)
        vT_sc[kb, hw:hw + ONES_ROWS, :] = jnp.ones((ONES_ROWS, blk), BF16)
    qbd_sc[...] = jnp.zeros(qbd_sc.shape, BF16)

    def query_block(qi, carry):
        _diff_attn_query_block(qi, lam_ref, q_ref, k_ref, g_ref, o_ref, qbd_sc, vT_sc, s_sc, p_sc, bm_sc, a_sc,
                               m_sc, acc_sc, blk=blk, lam_init=lam_init)
        return carry

    lax.fori_loop(0, n_q, query_block, 0)


def _diff_attn_query_block(qi, lam_ref, q_ref, k_ref, g_ref, o_ref, qbd_sc, vT_sc, s_sc, p_sc, bm_sc, a_sc,
                           m_sc, acc_sc, *, blk, lam_init):
    n_blocks = qi + 1
    q_rows = pl.ds(pl.multiple_of(qi * blk, blk), blk)

    qT = q_ref[0, q_rows, :].astype(F32).T.astype(BF16)
    qbd_sc[0:HEAD_DIM, 0:blk] = qT[0:HEAD_DIM]
    qbd_sc[HEAD_DIM:2 * HEAD_DIM, blk:2 * blk] = qT[HEAD_DIM:2 * HEAD_DIM]
    m_sc[...] = jnp.full(m_sc.shape, -jnp.inf, F32)
    acc_sc[...] = jnp.zeros(acc_sc.shape, F32)

    def key_block(t):
        return jnp.where(t == 0, qi, t - 1)

    def scores(t, slot, masked=False):
        k0 = pl.multiple_of(key_block(t) * blk, blk)
        sT = jnp.dot(k_ref[0, pl.ds(k0, blk), :], qbd_sc[...], preferred_element_type=F32)
        if masked:
            key = lax.broadcasted_iota(jnp.int32, sT.shape, 0)
            qry = lax.broadcasted_iota(jnp.int32, sT.shape, 1) & (blk - 1)
            sT = jnp.where(key <= qry, sT, -jnp.inf)
        s_sc[slot][...] = sT
        bm_sc[slot][...] = jnp.max(sT, axis=0, keepdims=True)

    def softmax(slot):
        sT = s_sc[slot][...]
        m_prev = m_sc[...]
        m_new = jnp.maximum(m_prev, bm_sc[slot][...])
        alpha = jnp.exp2(m_prev - m_new)
        pT = jnp.exp2(sT - m_new)
        p_sc[slot][...] = pT.astype(BF16)
        a_sc[slot][...] = alpha
        m_sc[...] = m_new

    def accumulate(t, slot):
        pv = jnp.dot(vT_sc[key_block(t)], p_sc[slot][...], preferred_element_type=F32)
        acc_sc[...] = a_sc[slot][...] * acc_sc[...] + pv

    scores(0, 0, masked=True)

    @pl.when(n_blocks == 1)
    def _():
        softmax(0)
        accumulate(0, 0)

    @pl.when(n_blocks >= 2)
    def _():
        scores(1, 1)
        softmax(0)
        n_steady = n_blocks - 2

        def tick_pair(i, carry):
            t = 1 + 2 * i
            softmax(1)
            accumulate(t - 1, 0)
            scores(t + 1, 0)
            softmax(0)
            accumulate(t, 1)
            scores(t + 2, 1)
            return carry

        lax.fori_loop(0, n_steady // 2, tick_pair, 0)
        last = n_blocks - 1

        @pl.when(n_steady % 2 == 1)
        def _():
            accumulate(last - 2, 0)
            scores(last, 0)
            softmax(1)
            accumulate(last - 1, 1)
            softmax(0)
            accumulate(last, 0)

        @pl.when(n_steady % 2 == 0)
        def _():
            accumulate(last - 1, 0)
            softmax(1)
            accumulate(last, 1)

    lp = lam_ref[...]
    lam = (jnp.exp(jnp.sum(lp[0:1] * lp[1:2], axis=-1, keepdims=True))
           - jnp.exp(jnp.sum(lp[2:3] * lp[3:4], axis=-1, keepdims=True)) + lam_init)
    hw = 2 * HEAD_DIM
    o_all = acc_sc[0:hw, :] * (1.0 / acc_sc[hw:hw + 1, :])
    oT = o_all[:, 0:blk] - lam * o_all[:, blk:2 * blk]
    inv_rms = lax.rsqrt(jnp.mean(oT * oT, axis=0, keepdims=True) + SUBLN_EPS)
    oT = oT * inv_rms * (g_ref[...] * (1.0 - lam_init))
    o_ref[0, q_rows, :] = oT.T.astype(o_ref.dtype)


def diff_attention(qkv, lam_params, subln_g, batch, layer_idx, *, blk=512):
    M, W3 = qkv.shape
    W = W3 // 3
    S = M // batch
    heads = W // (2 * HEAD_DIM)
    hw = 2 * HEAD_DIM
    lam_init = 0.8 - 0.6 * float(np.exp(-0.3 * layer_idx))
    view = qkv.reshape(batch, S, W3)
    out = pl.pallas_call(
        functools.partial(_diff_attn_kernel, blk=blk, lam_init=lam_init),
        grid=(batch, heads),
        in_specs=[
            pl.BlockSpec((4, HEAD_DIM), lambda b, h: (0, 0)),
            pl.BlockSpec((1, S, hw), lambda b, h: (b, 0, h)),
            pl.BlockSpec((1, S, hw), lambda b, h: (b, 0, heads + h)),
            pl.BlockSpec((1, S, hw), lambda b, h: (b, 0, 2 * heads + h)),
            pl.BlockSpec((hw, 1), lambda b, h: (0, 0)),
        ],
        out_specs=pl.BlockSpec((1, S, hw), lambda b, h: (b, 0, h)),
        out_shape=jax.ShapeDtypeStruct((batch, S, W), BF16),
        scratch_shapes=[
            pltpu.VMEM((hw, 2 * blk), BF16),
            pltpu.VMEM((S // blk, hw + ONES_ROWS, blk), BF16),
            pltpu.VMEM((blk, 2 * blk), F32), pltpu.VMEM((blk, 2 * blk), F32),
            pltpu.VMEM((blk, 2 * blk), BF16), pltpu.VMEM((blk, 2 * blk), BF16),
            pltpu.VMEM((1, 2 * blk), F32), pltpu.VMEM((1, 2 * blk), F32),
            pltpu.VMEM((1, 2 * blk), F32), pltpu.VMEM((1, 2 * blk), F32),
            pltpu.VMEM((1, 2 * blk), F32),
            pltpu.VMEM((hw + ONES_ROWS, 2 * blk), F32),
        ],
        compiler_params=_params("parallel", "parallel"),
        name="diff_attn",
    )(lam_params, view, view, view, subln_g.reshape(hw, 1))
    return out.reshape(M, W)


def _ffn(x, h, w_in, w_out, widx, g_post, g_next):
    act = matmul_swiglu(h, w_in, widx, tm=1024, tn=512)
    return matmul_resid(act, w_out, widx, x, g_post, g_next, 0.5, tm=512, tk=512)


def kernel(x, positions, norm_g, ffn_in, ffn_out, dil_w_in, dil_w_out, diff_w_in, diff_lambda,
           diff_subln_g, diff_w_out):
    B, S, D = x.shape
    depth = norm_g.shape[0]
    M = B * S
    x = x.reshape(M, D)
    g = norm_g.reshape(depth * 6, 1, D)
    ffn_in, ffn_out, dil_w_in, dil_w_out, diff_w_in, diff_w_out = (
        w.astype(BF16) for w in (ffn_in, ffn_out, dil_w_in, dil_w_out, diff_w_in, diff_w_out))
    tabs = rope_tables(positions)

    h = rms_cast(x, g[0])
    for l in range(depth):
        gl = g[6 * l:6 * l + 6]
        x, h = _ffn(x, h, ffn_in, ffn_out, (l, 0), gl[1], gl[2])
        if l % 2 == 0:
            parts = []
            for grp, (_, dil) in enumerate(DIL_PATTERNS):
                qkv = matmul_rope(h, dil_w_in, (l // 2,), tabs, tm=1024, tn=DIL_GROUP_WIDTH,
                                  part_width=DIL_GROUP_WIDTH, col0=grp, col_stride=DIL_GROUPS, dilation=dil)
                qkv = qkv.reshape(M // (BAND_BLOCK * dil), dil, BAND_BLOCK, 3 * DIL_GROUP_WIDTH)
                parts.append(dilated_group_attention(qkv, grp, dil, B))
            mixed = dilated_mix([p[0] for p in parts], [p[1] for p in parts])
            x, h = matmul_resid(mixed, dil_w_out, (l // 2,), x, gl[3], gl[4], 1.0, tm=512, tk=768)
        else:
            qkv = matmul_rope(h, diff_w_in, (l // 2,), tabs, tm=1024, tn=1024,
                              part_width=diff_w_in.shape[-1] // 3)
            att = diff_attention(qkv, diff_lambda[l // 2], diff_subln_g[l // 2], B, l)
            x, h = matmul_resid(att, diff_w_out, (l // 2,), x, gl[3], gl[4], 1.0, tm=512, tk=512)
        g_next = g[6 * l + 6] if l + 1 < depth else None
        x, h = _ffn(x, h, ffn_in, ffn_out, (l, 1), gl[5], g_next)
    return x.reshape(B, S, D)
```

```python
import functools

import numpy as np
import jax
import jax.numpy as jnp
from jax import lax
from jax.experimental import pallas as pl
from jax.experimental.pallas import tpu as pltpu

HEAD_DIM = 128
ROPE_THETA = 500000.0
ROPE_DIM = HEAD_DIM // 4
ROPE_HALF = ROPE_DIM // 2
NORM_EPS = 1e-6
SUBLN_EPS = 1e-5
DIL_PATTERNS = ((128, 1), (512, 4), (2048, 16))
DIL_GROUPS = len(DIL_PATTERNS)
DIL_HEADS_PER_GROUP = 10
DIL_HEADS = DIL_GROUPS * DIL_HEADS_PER_GROUP
DIL_WIDTH = DIL_HEADS * HEAD_DIM
DIL_GROUP_WIDTH = DIL_HEADS_PER_GROUP * HEAD_DIM
BAND_BLOCK = 128
LANES = 128
LN2 = float(np.log(2.0))
Q_PRESCALE = float(HEAD_DIM ** -0.5 / np.log(2.0))
assert all(window // dil == BAND_BLOCK for window, dil in DIL_PATTERNS)

VMEM_LIMIT_BYTES = 56 * 1024 * 1024

F32 = jnp.float32
BF16 = jnp.bfloat16


def _params(*sem):
    return pltpu.CompilerParams(dimension_semantics=sem, vmem_limit_bytes=VMEM_LIMIT_BYTES)


def _rms(x, eps):
    return x * lax.rsqrt(jnp.mean(x * x, axis=-1, keepdims=True) + eps)


def _rms_cast_kernel(x_ref, g_ref, o_ref):
    o_ref[...] = (_rms(x_ref[...], NORM_EPS) * g_ref[...]).astype(o_ref.dtype)


def rms_cast(x, g, *, tm=256):
    M, D = x.shape
    return pl.pallas_call(
        _rms_cast_kernel,
        grid=(M // tm,),
        in_specs=[pl.BlockSpec((tm, D), lambda i: (i, 0)), pl.BlockSpec((1, D), lambda i: (0, 0))],
        out_specs=pl.BlockSpec((tm, D), lambda i: (i, 0)),
        out_shape=jax.ShapeDtypeStruct((M, D), BF16),
        compiler_params=_params("parallel"),
        name="rms_cast",
    )(x, g)


def _weight_spec(w, widx, tn, col=lambda j: j):
    K = w.shape[-2]
    return pl.BlockSpec((None,) * len(widx) + (K, tn), lambda i, j: (*widx, 0, col(j)))


RESID_COL_CHUNK = 1024
RESID_ROW_CHUNK = 128


def _mm_resid_kernel(a_ref, w_ref, x_hbm, gp_ref, gn_ref, xo_hbm, *rest, coef, with_h):
    if with_h:
        ho_hbm, acc_sc, x_sc, h_sc, x_sem, xo_sem, h_sem = rest
    else:
        acc_sc, x_sc, x_sem, xo_sem = rest
    k = pl.program_id(1)
    tm = acc_sc.shape[0]
    row0 = pl.multiple_of(pl.program_id(0) * tm, tm)
    x_copy = pltpu.make_async_copy(x_hbm.at[pl.ds(row0, tm)], x_sc, x_sem)

    def partial_products():
        a = a_ref[...]
        for n0 in range(0, acc_sc.shape[1], RESID_COL_CHUNK):
            cols = slice(n0, n0 + RESID_COL_CHUNK)
            yield cols, jnp.dot(a, w_ref[:, cols], preferred_element_type=F32)

    @pl.when(k == 0)
    def _():
        x_copy.start()
        for cols, part in partial_products():
            acc_sc[:, cols] = part

    @pl.when(k > 0)
    def _():
        for cols, part in partial_products():
            acc_sc[:, cols] += part

    @pl.when(k == pl.num_programs(1) - 1)
    def _():
        x_copy.wait()
        gain = coef * gp_ref[...]
        xo_copies, h_copies = [], []
        for c in range(tm // RESID_ROW_CHUNK):
            rows = slice(c * RESID_ROW_CHUNK, (c + 1) * RESID_ROW_CHUNK)
            out_rows = pl.ds(row0 + c * RESID_ROW_CHUNK, RESID_ROW_CHUNK)
            xn = x_sc[rows, :] + _rms(acc_sc[rows, :], NORM_EPS) * gain
            acc_sc[rows, :] = xn
            xo_copies.append(pltpu.make_async_copy(acc_sc.at[rows], xo_hbm.at[out_rows], xo_sem.at[c]))
            xo_copies[-1].start()
            if with_h:
                slot = c % 2
                if c >= 2:
                    h_copies[c - 2].wait()
                h_sc[slot] = (_rms(xn, NORM_EPS) * gn_ref[...]).astype(h_sc.dtype)
                h_copies.append(pltpu.make_async_copy(h_sc.at[slot], ho_hbm.at[out_rows], h_sem.at[slot]))
                h_copies[-1].start()
        for cp in h_copies[-2:] + xo_copies:
            cp.wait()


def matmul_resid(a, w, widx, x, g_post, g_next, coef, *, tm, tk):
    M, K = a.shape
    D = w.shape[-1]
    vec = pl.BlockSpec((1, D), lambda i, k: (0, 0))
    hbm = pl.BlockSpec(memory_space=pl.ANY)
    with_h = g_next is not None
    n_chunks = tm // RESID_ROW_CHUNK
    scratch = [pltpu.VMEM((tm, D), F32), pltpu.VMEM((tm, D), F32)]
    sems = [pltpu.SemaphoreType.DMA(()), pltpu.SemaphoreType.DMA((n_chunks,))]
    if with_h:
        scratch.append(pltpu.VMEM((2, RESID_ROW_CHUNK, D), BF16))
        sems.append(pltpu.SemaphoreType.DMA((2,)))
    out = pl.pallas_call(
        functools.partial(_mm_resid_kernel, coef=coef, with_h=with_h),
        grid=(M // tm, K // tk),
        in_specs=[pl.BlockSpec((tm, tk), lambda i, k: (i, k)),
                  pl.BlockSpec((None,) * len(widx) + (tk, D), lambda i, k: (*widx, k, 0)),
                  hbm, vec, vec],
        out_specs=[hbm, hbm] if with_h else [hbm],
        out_shape=[jax.ShapeDtypeStruct((M, D), F32)] + ([jax.ShapeDtypeStruct((M, D), BF16)] if with_h else []),
        scratch_shapes=scratch + sems,
        compiler_params=_params("arbitrary", "arbitrary"),
        name="matmul_resid",
    )(a, w, x, g_post, g_next if with_h else g_post)
    return (out[0], out[1]) if with_h else (out[0], None)


def _mm_swiglu_kernel(a_ref, wg_ref, wu_ref, o_ref):
    a = a_ref[...]
    gate = jnp.dot(a, wg_ref[...], preferred_element_type=F32)
    up = jnp.dot(a, wu_ref[...], preferred_element_type=F32)
    o_ref[...] = (gate * jax.nn.sigmoid(gate) * up).astype(o_ref.dtype)


def matmul_swiglu(a, w_in, widx, *, tm, tn):
    M, K = a.shape
    F = w_in.shape[-1] // 2
    nj = F // tn
    return pl.pallas_call(
        _mm_swiglu_kernel,
        grid=(M // tm, nj),
        in_specs=[
            pl.BlockSpec((tm, K), lambda i, j: (i, 0)),
            _weight_spec(w_in, widx, tn),
            _weight_spec(w_in, widx, tn, lambda j: j + nj),
        ],
        out_specs=pl.BlockSpec((tm, tn), lambda i, j: (i, j)),
        out_shape=jax.ShapeDtypeStruct((M, F), BF16),
        compiler_params=_params("parallel", "arbitrary"),
        name="matmul_swiglu",
    )(a, w_in, w_in)


def _rope_table_kernel(pos_ref, invf_ref, c_ref, s1_ref, s2_ref):
    ang = pos_ref[...].astype(F32) * invf_ref[...]
    lane = lax.broadcasted_iota(jnp.int32, ang.shape, 1)
    cos = jnp.cos(ang)
    sin = jnp.sin(ang)
    c_ref[...] = jnp.where(lane < ROPE_DIM, cos, 1.0)
    s1_ref[...] = jnp.where(lane < ROPE_HALF, -sin, 0.0)
    s2_ref[...] = jnp.where((lane >= ROPE_HALF) & (lane < ROPE_DIM), sin, 0.0)


def rope_tables(positions, *, tm=512):
    M = positions.size
    inv_freq = 1.0 / (ROPE_THETA ** (jnp.arange(0, ROPE_DIM, 2, dtype=F32) / ROPE_DIM))
    invf = jnp.concatenate([inv_freq, inv_freq, jnp.zeros((LANES - ROPE_DIM,), F32)]).reshape(1, LANES)
    tab = jax.ShapeDtypeStruct((M, LANES), F32)
    row = pl.BlockSpec((tm, LANES), lambda i: (i, 0))
    return pl.pallas_call(
        _rope_table_kernel,
        grid=(M // tm,),
        in_specs=[pl.BlockSpec((tm, 1), lambda i: (i, 0)), pl.BlockSpec((1, LANES), lambda i: (0, 0))],
        out_specs=[row, row, row],
        out_shape=[tab, tab, tab],
        compiler_params=_params("parallel"),
        name="rope_tables",
    )(positions.reshape(M, 1), invf)


ROPE_ROW_CHUNK = 256


def _mm_rope_kernel(a_ref, w_ref, c_ref, s1_ref, s2_ref, o_ref, *stage_refs, tiles_per_part, dilation):
    part = pl.program_id(1) // tiles_per_part
    is_rope = part < 2
    mult = jnp.where(part == 0, Q_PRESCALE, 1.0)
    for r0 in range(0, a_ref.shape[0], ROPE_ROW_CHUNK):
        rows = slice(r0, r0 + ROPE_ROW_CHUNK)
        acc = jnp.dot(a_ref[rows, :], w_ref[...], preferred_element_type=F32)
        c = jnp.where(is_rope, c_ref[rows, :] * mult, 1.0)
        s1 = jnp.where(is_rope, s1_ref[rows, :] * mult, 0.0)
        s2 = jnp.where(is_rope, s2_ref[rows, :] * mult, 0.0)
        stage = stage_refs[(r0 // ROPE_ROW_CHUNK) % 2] if dilation > 1 else None
        for h in range(acc.shape[1] // HEAD_DIM):
            sl = slice(h * HEAD_DIM, (h + 1) * HEAD_DIM)
            t = acc[:, sl]
            fwd = pltpu.roll(t, HEAD_DIM - ROPE_HALF, 1)
            bwd = pltpu.roll(t, ROPE_HALF, 1)
            roped = t * c + fwd * s1 + bwd * s2
            if dilation == 1:
                o_ref[rows, sl] = roped.astype(o_ref.dtype)
            else:
                stage[h] = roped
        if dilation > 1:
            n = ROPE_ROW_CHUNK // dilation
            sup, i0 = divmod(r0 // dilation, o_ref.shape[2])
            for h in range(acc.shape[1] // HEAD_DIM):
                sl = slice(h * HEAD_DIM, (h + 1) * HEAD_DIM)
                for r in range(dilation):
                    o_ref[sup, r, i0:i0 + n, sl] = stage[h, pl.ds(r, n, stride=dilation), :].astype(o_ref.dtype)


def matmul_rope(a, w, widx, tabs, *, tm, tn, part_width, col0=0, col_stride=1, dilation=1):
    M, K = a.shape
    tiles_per_part = part_width // tn
    tab = pl.BlockSpec((tm, LANES), lambda i, j: (i, 0))

    def col(j):
        return col0 + (j // tiles_per_part) * col_stride * tiles_per_part + j % tiles_per_part

    if dilation == 1:
        out_spec = pl.BlockSpec((tm, tn), lambda i, j: (i, j))
        out_shape = jax.ShapeDtypeStruct((M, 3 * part_width), BF16)
        scratch = []
    else:
        span = BAND_BLOCK * dilation
        tiles_per_span = max(span // tm, 1)
        out_spec = pl.BlockSpec((max(tm // span, 1), dilation, min(BAND_BLOCK, tm // dilation), tn),
                                lambda i, j: (i // tiles_per_span, 0, i % tiles_per_span, j))
        out_shape = jax.ShapeDtypeStruct((M // span, dilation, BAND_BLOCK, 3 * part_width), BF16)
        scratch = [pltpu.VMEM((tn // HEAD_DIM, ROPE_ROW_CHUNK, HEAD_DIM), F32)] * 2
    return pl.pallas_call(
        functools.partial(_mm_rope_kernel, tiles_per_part=tiles_per_part, dilation=dilation),
        grid=(M // tm, 3 * tiles_per_part),
        in_specs=[pl.BlockSpec((tm, K), lambda i, j: (i, 0)), _weight_spec(w, widx, tn, col), tab, tab, tab],
        out_specs=out_spec,
        out_shape=out_shape,
        scratch_shapes=scratch,
        compiler_params=_params("parallel", "arbitrary"),
        name="matmul_rope",
    )(a, w, *tabs)


_NT = (((1,), (1,)), ((), ()))


def _dil_attn_kernel(q_ref, kp_ref, kc_ref, vp_ref, vc_ref, o_ref, lse_ref):
    n = pl.program_id(2)
    qi = lax.broadcasted_iota(jnp.int32, (BAND_BLOCK, 2 * BAND_BLOCK), 0)
    kj = lax.broadcasted_iota(jnp.int32, (BAND_BLOCK, 2 * BAND_BLOCK), 1)
    first_key = jnp.where(n > 0, qi, jnp.maximum(qi, BAND_BLOCK))
    mask = (kj >= first_key) & (kj <= qi + BAND_BLOCK)
    lane = lax.broadcasted_iota(jnp.int32, (BAND_BLOCK, LANES), 1)
    lse_tile = jnp.zeros((BAND_BLOCK, LANES), F32)
    for j in range(DIL_HEADS_PER_GROUP):
        sl = slice(j * HEAD_DIM, (j + 1) * HEAD_DIM)
        k = jnp.concatenate([kp_ref[:, sl], kc_ref[:, sl]], axis=0)
        v = jnp.concatenate([vp_ref[:, sl], vc_ref[:, sl]], axis=0)
        s = lax.dot_general(q_ref[:, sl], k, _NT, preferred_element_type=F32)
        s = jnp.where(mask, s, -jnp.inf)
        m = jnp.max(s, axis=-1, keepdims=True)
        p = jnp.exp2(s - m)
        den = jnp.sum(p, axis=-1, keepdims=True)
        o = jnp.dot(p.astype(BF16), v, preferred_element_type=F32)
        o_ref[0, :, sl] = o * (1.0 / den)
        lse_tile = jnp.where(lane == j, m * LN2 + jnp.log(den), lse_tile)
    lse_ref[0] = lse_tile


def dilated_group_attention(qkv, group, dilation, batch):
    M = qkv.shape[0] * dilation * BAND_BLOCK
    S = M // batch
    L = S // dilation
    nb = L // BAND_BLOCK
    blk = (1, BAND_BLOCK, DIL_GROUP_WIDTH)

    def spec(part, prev):
        def index(b, r, n):
            return (b * nb + (jnp.maximum(n - 1, 0) if prev else n), r, 0, part)
        return pl.BlockSpec((None, None, BAND_BLOCK, DIL_GROUP_WIDTH), index)

    o, lse = pl.pallas_call(
        _dil_attn_kernel,
        grid=(batch, dilation, nb),
        in_specs=[spec(0, False), spec(1, True), spec(1, False), spec(2, True), spec(2, False)],
        out_specs=[pl.BlockSpec(blk, lambda b, r, n: (b, n, r)),
                   pl.BlockSpec((1, BAND_BLOCK, LANES), lambda b, r, n: (b, n, r))],
        out_shape=[jax.ShapeDtypeStruct((batch, L, dilation * DIL_GROUP_WIDTH), F32),
                   jax.ShapeDtypeStruct((batch, L, dilation * LANES), F32)],
        compiler_params=_params("parallel", "parallel", "arbitrary"),
        name=f"dil_attn_g{group}",
    )(qkv, qkv, qkv, qkv, qkv)
    return o.reshape(M, DIL_GROUP_WIDTH), lse.reshape(M, LANES)


def _dil_mix_kernel(o0_ref, o1_ref, o2_ref, l0_ref, l1_ref, l2_ref, out_ref):
    lses = [l0_ref[...], l1_ref[...], l2_ref[...]]
    mx = jnp.maximum(jnp.maximum(lses[0], lses[1]), lses[2])
    es = [jnp.exp(l - mx) for l in lses]
    tot = es[0] + es[1] + es[2]
    for g, o_ref in enumerate((o0_ref, o1_ref, o2_ref)):
        alpha = es[g] / tot
        for j in range(DIL_HEADS_PER_GROUP):
            src = slice(j * HEAD_DIM, (j + 1) * HEAD_DIM)
            col = (g * DIL_HEADS_PER_GROUP + j) * HEAD_DIM
            out_ref[:, col:col + HEAD_DIM] = (alpha[:, j:j + 1] * o_ref[:, src]).astype(out_ref.dtype)


def dilated_mix(outs, lses, *, tm=256):
    M = outs[0].shape[0]
    o_spec = pl.BlockSpec((tm, DIL_GROUP_WIDTH), lambda i: (i, 0))
    l_spec = pl.BlockSpec((tm, LANES), lambda i: (i, 0))
    return pl.pallas_call(
        _dil_mix_kernel,
        grid=(M // tm,),
        in_specs=[o_spec] * 3 + [l_spec] * 3,
        out_specs=pl.BlockSpec((tm, DIL_WIDTH), lambda i: (i, 0)),
        out_shape=jax.ShapeDtypeStruct((M, DIL_WIDTH), BF16),
        compiler_params=_params("parallel"),
        name="dil_mix",
    )(*outs, *lses)


ONES_ROWS = 16


def _diff_attn_kernel(lam_ref, q_ref, k_ref, v_ref, g_ref, o_ref, qbd_sc, vT_sc, s0_sc, s1_sc, p0_sc, p1_sc,
                      bm0_sc, bm1_sc, a0_sc, a1_sc, m_sc, acc_sc, *, blk, kblk, lam_init):
    s_sc = (s0_sc, s1_sc)
    p_sc = (p0_sc, p1_sc)
    bm_sc = (bm0_sc, bm1_sc)
    a_sc = (a0_sc, a1_sc)
    seq = v_ref.shape[1]

    hw = v_ref.shape[2]
    for kb in range(seq // kblk):
        vT_sc[kb, 0:hw, :] = v_ref[0, kb * kblk:(kb + 1) * kblk, :].astype(F32).T.astype(BF16)
        vT_sc[kb, hw:hw + ONES_ROWS, :] = jnp.ones((ONES_ROWS, kblk), BF16)
    qbd_sc[...] = jnp.zeros(qbd_sc.shape, BF16)

    def query_block(qi, carry):
        _diff_attn_query_block(qi, lam_ref, q_ref, k_ref, g_ref, o_ref, qbd_sc, vT_sc, s_sc, p_sc, bm_sc, a_sc,
                               m_sc, acc_sc, blk=blk, kblk=kblk, lam_init=lam_init)
        return carry

    lax.fori_loop(0, seq // blk, query_block, 0)


def _diff_attn_query_block(qi, lam_ref, q_ref, k_ref, g_ref, o_ref, qbd_sc, vT_sc, s_sc, p_sc, bm_sc, a_sc,
                           m_sc, acc_sc, *, blk, kblk, lam_init):
    n_diag = blk // kblk
    n_blocks = n_diag * (qi + 1)
    q_rows = pl.ds(pl.multiple_of(qi * blk, blk), blk)

    qT = q_ref[0, q_rows, :].astype(F32).T.astype(BF16)
    qbd_sc[0:HEAD_DIM, 0:blk] = qT[0:HEAD_DIM]
    qbd_sc[HEAD_DIM:2 * HEAD_DIM, blk:2 * blk] = qT[HEAD_DIM:2 * HEAD_DIM]
    m_sc[...] = jnp.full(m_sc.shape, -jnp.inf, F32)
    acc_sc[...] = jnp.zeros(acc_sc.shape, F32)

    def key_block(t):
        return jnp.where(t < n_diag, n_diag * qi + t, t - n_diag)

    def scores(t, slot):
        k0 = pl.multiple_of(key_block(t) * kblk, kblk)
        sT = jnp.dot(k_ref[0, pl.ds(k0, kblk), :], qbd_sc[...], preferred_element_type=F32)
        if isinstance(t, int) and t < n_diag:
            key = t * kblk + lax.broadcasted_iota(jnp.int32, sT.shape, 0)
            qry = lax.broadcasted_iota(jnp.int32, sT.shape, 1) & (blk - 1)
            sT = jnp.where(key <= qry, sT, -jnp.inf)
        s_sc[slot][...] = sT
        bm_sc[slot][...] = jnp.max(sT, axis=0, keepdims=True)

    def softmax(slot):
        sT = s_sc[slot][...]
        m_prev = m_sc[...]
        m_new = jnp.maximum(m_prev, bm_sc[slot][...])
        alpha = jnp.exp2(m_prev - m_new)
        pT = jnp.exp2(sT - m_new)
        p_sc[slot][...] = pT.astype(BF16)
        a_sc[slot][...] = alpha
        m_sc[...] = m_new

    def accumulate(t, slot):
        pv = jnp.dot(vT_sc[key_block(t)], p_sc[slot][...], preferred_element_type=F32)
        acc_sc[...] = a_sc[slot][...] * acc_sc[...] + pv

    assert n_diag in (1, 2)
    scores(0, 0)

    @pl.when(n_blocks == 1)
    def _():
        softmax(0)
        accumulate(0, 0)

    @pl.when(n_blocks >= 2)
    def _():
        scores(1, 1)
        softmax(0)
        n_steady = n_blocks - 2

        def tick_pair(i, carry):
            t = 1 + 2 * i
            softmax(1)
            accumulate(t - 1, 0)
            scores(t + 1, 0)
            softmax(0)
            accumulate(t, 1)
            scores(t + 2, 1)
            return carry

        lax.fori_loop(0, n_steady // 2, tick_pair, 0)
        last = n_blocks - 1

        @pl.when(n_steady % 2 == 1)
        def _():
            accumulate(last - 2, 0)
            scores(last, 0)
            softmax(1)
            accumulate(last - 1, 1)
            softmax(0)
            accumulate(last, 0)

        @pl.when(n_steady % 2 == 0)
        def _():
            accumulate(last - 1, 0)
            softmax(1)
            accumulate(last, 1)

    lp = lam_ref[...]
    lam = (jnp.exp(jnp.sum(lp[0:1] * lp[1:2], axis=-1, keepdims=True))
           - jnp.exp(jnp.sum(lp[2:3] * lp[3:4], axis=-1, keepdims=True)) + lam_init)
    hw = 2 * HEAD_DIM
    o_all = acc_sc[0:hw, :] * (1.0 / acc_sc[hw:hw + 1, :])
    oT = o_all[:, 0:blk] - lam * o_all[:, blk:2 * blk]
    inv_rms = lax.rsqrt(jnp.mean(oT * oT, axis=0, keepdims=True) + SUBLN_EPS)
    oT = oT * inv_rms * (g_ref[...] * (1.0 - lam_init))
    o_ref[0, q_rows, :] = oT.T.astype(o_ref.dtype)


def diff_attention(qkv, lam_params, subln_g, batch, layer_idx, *, blk=512, kblk=512):
    M, W3 = qkv.shape
    W = W3 // 3
    S = M // batch
    heads = W // (2 * HEAD_DIM)
    hw = 2 * HEAD_DIM
    lam_init = 0.8 - 0.6 * float(np.exp(-0.3 * layer_idx))
    view = qkv.reshape(batch, S, W3)
    out = pl.pallas_call(
        functools.partial(_diff_attn_kernel, blk=blk, kblk=kblk, lam_init=lam_init),
        grid=(batch, heads),
        in_specs=[
            pl.BlockSpec((4, HEAD_DIM), lambda b, h: (0, 0)),
            pl.BlockSpec((1, S, hw), lambda b, h: (b, 0, h)),
            pl.BlockSpec((1, S, hw), lambda b, h: (b, 0, heads + h)),
            pl.BlockSpec((1, S, hw), lambda b, h: (b, 0, 2 * heads + h)),
            pl.BlockSpec((hw, 1), lambda b, h: (0, 0)),
        ],
        out_specs=pl.BlockSpec((1, S, hw), lambda b, h: (b, 0, h)),
        out_shape=jax.ShapeDtypeStruct((batch, S, W), BF16),
        scratch_shapes=[
            pltpu.VMEM((hw, 2 * blk), BF16),
            pltpu.VMEM((S // kblk, hw + ONES_ROWS, kblk), BF16),
            pltpu.VMEM((kblk, 2 * blk), F32), pltpu.VMEM((kblk, 2 * blk), F32),
            pltpu.VMEM((kblk, 2 * blk), BF16), pltpu.VMEM((kblk, 2 * blk), BF16),
            pltpu.VMEM((1, 2 * blk), F32), pltpu.VMEM((1, 2 * blk), F32),
            pltpu.VMEM((1, 2 * blk), F32), pltpu.VMEM((1, 2 * blk), F32),
            pltpu.VMEM((1, 2 * blk), F32),
            pltpu.VMEM((hw + ONES_ROWS, 2 * blk), F32),
        ],
        compiler_params=_params("parallel", "parallel"),
        name="diff_attn",
    )(lam_params, view, view, view, subln_g.reshape(hw, 1))
    return out.reshape(M, W)


def _ffn(x, h, w_in, w_out, widx, g_post, g_next):
    act = matmul_swiglu(h, w_in, widx, tm=1024, tn=512)
    return matmul_resid(act, w_out, widx, x, g_post, g_next, 0.5, tm=1024, tk=512)


def kernel(x, positions, norm_g, ffn_in, ffn_out, dil_w_in, dil_w_out, diff_w_in, diff_lambda,
           diff_subln_g, diff_w_out):
    B, S, D = x.shape
    depth = norm_g.shape[0]
    M = B * S
    x = x.reshape(M, D)
    g = norm_g.reshape(depth * 6, 1, D)
    ffn_in, ffn_out, dil_w_in, dil_w_out, diff_w_in, diff_w_out = (
        w.astype(BF16) for w in (ffn_in, ffn_out, dil_w_in, dil_w_out, diff_w_in, diff_w_out))
    tabs = rope_tables(positions)

    h = rms_cast(x, g[0])
    for l in range(depth):
        gl = g[6 * l:6 * l + 6]
        x, h = _ffn(x, h, ffn_in, ffn_out, (l, 0), gl[1], gl[2])
        if l % 2 == 0:
            parts = []
            for grp, (_, dil) in enumerate(DIL_PATTERNS):
                qkv = matmul_rope(h, dil_w_in, (l // 2,), tabs, tm=1024, tn=DIL_GROUP_WIDTH,
                                  part_width=DIL_GROUP_WIDTH, col0=grp, col_stride=DIL_GROUPS, dilation=dil)
                qkv = qkv.reshape(M // (BAND_BLOCK * dil), dil, BAND_BLOCK, 3 * DIL_GROUP_WIDTH)
                parts.append(dilated_group_attention(qkv, grp, dil, B))
            mixed = dilated_mix([p[0] for p in parts], [p[1] for p in parts])
            x, h = matmul_resid(mixed, dil_w_out, (l // 2,), x, gl[3], gl[4], 1.0, tm=1024, tk=768)
        else:
            qkv = matmul_rope(h, diff_w_in, (l // 2,), tabs, tm=1024, tn=1024,
                              part_width=diff_w_in.shape[-1] // 3)
            att = diff_attention(qkv, diff_lambda[l // 2], diff_subln_g[l // 2], B, l)
            x, h = matmul_resid(att, diff_w_out, (l // 2,), x, gl[3], gl[4], 1.0, tm=1024, tk=512)
        g_next = g[6 * l + 6] if l + 1 < depth else None
        x, h = _ffn(x, h, ffn_in, ffn_out, (l, 1), gl[5], g_next)
    return x.reshape(B, S, D)
```

```python
import functools

import numpy as np
import jax
import jax.numpy as jnp
from jax import lax
from jax.experimental import pallas as pl
from jax.experimental.pallas import tpu as pltpu

HEAD_DIM = 128
ROPE_THETA = 500000.0
ROPE_DIM = HEAD_DIM // 4
ROPE_HALF = ROPE_DIM // 2
NORM_EPS = 1e-6
SUBLN_EPS = 1e-5
DIL_PATTERNS = ((128, 1), (512, 4), (2048, 16))
DIL_GROUPS = len(DIL_PATTERNS)
DIL_HEADS_PER_GROUP = 10
DIL_HEADS = DIL_GROUPS * DIL_HEADS_PER_GROUP
DIL_WIDTH = DIL_HEADS * HEAD_DIM
DIL_GROUP_WIDTH = DIL_HEADS_PER_GROUP * HEAD_DIM
BAND_BLOCK = 128
LANES = 128
LN2 = float(np.log(2.0))
Q_PRESCALE = float(HEAD_DIM ** -0.5 / np.log(2.0))
assert all(window // dil == BAND_BLOCK for window, dil in DIL_PATTERNS)

VMEM_LIMIT_BYTES = 56 * 1024 * 1024

F32 = jnp.float32
BF16 = jnp.bfloat16


def _params(*sem):
    return pltpu.CompilerParams(dimension_semantics=sem, vmem_limit_bytes=VMEM_LIMIT_BYTES)


def _rms(x, eps):
    return x * lax.rsqrt(jnp.mean(x * x, axis=-1, keepdims=True) + eps)


def _inv_rms(x):
    return lax.rsqrt(jnp.mean(x * x, axis=-1, keepdims=True) + NORM_EPS)


def _rms_cast_kernel(x_ref, g_ref, o_ref):
    o_ref[...] = (_rms(x_ref[...], NORM_EPS) * g_ref[...]).astype(o_ref.dtype)


def rms_cast(x, g, *, tm=256):
    M, D = x.shape
    return pl.pallas_call(
        _rms_cast_kernel,
        grid=(M // tm,),
        in_specs=[pl.BlockSpec((tm, D), lambda i: (i, 0)), pl.BlockSpec((1, D), lambda i: (0, 0))],
        out_specs=pl.BlockSpec((tm, D), lambda i: (i, 0)),
        out_shape=jax.ShapeDtypeStruct((M, D), BF16),
        compiler_params=_params("parallel"),
        name="rms_cast",
    )(x, g)


def _weight_spec(w, widx, tn, col=lambda j: j):
    K = w.shape[-2]
    return pl.BlockSpec((None,) * len(widx) + (K, tn), lambda i, j: (*widx, 0, col(j)))


RESID_COL_CHUNK = 1024
RESID_ROW_CHUNK = 128


def _mm_resid_kernel(a_ref, w_ref, x_hbm, gp_ref, gn_ref, xo_hbm, *rest, coef, with_h):
    if with_h:
        ho_hbm, acc_sc, x_sc, h_sc, x_sem, xo_sem, h_sem = rest
    else:
        acc_sc, x_sc, x_sem, xo_sem = rest
    k = pl.program_id(1)
    tm = acc_sc.shape[0]
    row0 = pl.multiple_of(pl.program_id(0) * tm, tm)
    x_copy = pltpu.make_async_copy(x_hbm.at[pl.ds(row0, tm)], x_sc, x_sem)

    def partial_products():
        a = a_ref[...]
        for n0 in range(0, acc_sc.shape[1], RESID_COL_CHUNK):
            cols = slice(n0, n0 + RESID_COL_CHUNK)
            yield cols, jnp.dot(a, w_ref[:, cols], preferred_element_type=F32)

    @pl.when(k == 0)
    def _():
        x_copy.start()
        for cols, part in partial_products():
            acc_sc[:, cols] = part

    @pl.when(k > 0)
    def _():
        for cols, part in partial_products():
            acc_sc[:, cols] += part

    @pl.when(k == pl.num_programs(1) - 1)
    def _():
        x_copy.wait()
        gain = coef * gp_ref[...]
        xo_copies, h_copies = [], []
        for c in range(tm // RESID_ROW_CHUNK):
            rows = slice(c * RESID_ROW_CHUNK, (c + 1) * RESID_ROW_CHUNK)
            out_rows = pl.ds(row0 + c * RESID_ROW_CHUNK, RESID_ROW_CHUNK)
            acc_sc[rows, :] = x_sc[rows, :] + acc_sc[rows, :] * _inv_rms(acc_sc[rows, :]) * gain
            xo_copies.append(pltpu.make_async_copy(acc_sc.at[rows], xo_hbm.at[out_rows], xo_sem.at[c]))
            xo_copies[-1].start()
            if with_h:
                slot = c % 2
                if c >= 2:
                    h_copies[c - 2].wait()
                h_sc[slot] = (acc_sc[rows, :] * _inv_rms(acc_sc[rows, :]) * gn_ref[...]).astype(h_sc.dtype)
                h_copies.append(pltpu.make_async_copy(h_sc.at[slot], ho_hbm.at[out_rows], h_sem.at[slot]))
                h_copies[-1].start()
        for cp in h_copies[-2:] + xo_copies:
            cp.wait()


def matmul_resid(a, w, widx, x, g_post, g_next, coef, *, tm, tk):
    M, K = a.shape
    D = w.shape[-1]
    vec = pl.BlockSpec((1, D), lambda i, k: (0, 0))
    hbm = pl.BlockSpec(memory_space=pl.ANY)
    with_h = g_next is not None
    n_chunks = tm // RESID_ROW_CHUNK
    scratch = [pltpu.VMEM((tm, D), F32), pltpu.VMEM((tm, D), F32)]
    sems = [pltpu.SemaphoreType.DMA(()), pltpu.SemaphoreType.DMA((n_chunks,))]
    if with_h:
        scratch.append(pltpu.VMEM((2, RESID_ROW_CHUNK, D), BF16))
        sems.append(pltpu.SemaphoreType.DMA((2,)))
    out = pl.pallas_call(
        functools.partial(_mm_resid_kernel, coef=coef, with_h=with_h),
        grid=(M // tm, K // tk),
        in_specs=[pl.BlockSpec((tm, tk), lambda i, k: (i, k)),
                  pl.BlockSpec((None,) * len(widx) + (tk, D), lambda i, k: (*widx, k, 0)),
                  hbm, vec, vec],
        out_specs=[hbm, hbm] if with_h else [hbm],
        out_shape=[jax.ShapeDtypeStruct((M, D), F32)] + ([jax.ShapeDtypeStruct((M, D), BF16)] if with_h else []),
        scratch_shapes=scratch + sems,
        compiler_params=_params("arbitrary", "arbitrary"),
        name="matmul_resid",
    )(a, w, x, g_post, g_next if with_h else g_post)
    return (out[0], out[1]) if with_h else (out[0], None)


def _mm_swiglu_kernel(a_ref, wg_ref, wu_ref, o_ref):
    a = a_ref[...]
    gate = jnp.dot(a, wg_ref[...], preferred_element_type=F32)
    up = jnp.dot(a, wu_ref[...], preferred_element_type=F32)
    o_ref[...] = (gate * jax.nn.sigmoid(gate) * up).astype(o_ref.dtype)


def matmul_swiglu(a, w_in, widx, *, tm, tn):
    M, K = a.shape
    F = w_in.shape[-1] // 2
    nj = F // tn
    return pl.pallas_call(
        _mm_swiglu_kernel,
        grid=(M // tm, nj),
        in_specs=[
            pl.BlockSpec((tm, K), lambda i, j: (i, 0)),
            _weight_spec(w_in, widx, tn),
            _weight_spec(w_in, widx, tn, lambda j: j + nj),
        ],
        out_specs=pl.BlockSpec((tm, tn), lambda i, j: (i, j)),
        out_shape=jax.ShapeDtypeStruct((M, F), BF16),
        compiler_params=_params("parallel", "arbitrary"),
        name="matmul_swiglu",
    )(a, w_in, w_in)


def _rope_table_kernel(pos_ref, invf_ref, c_ref, s1_ref, s2_ref):
    ang = pos_ref[...].astype(F32) * invf_ref[...]
    lane = lax.broadcasted_iota(jnp.int32, ang.shape, 1)
    cos = jnp.cos(ang)
    sin = jnp.sin(ang)
    c_ref[...] = jnp.where(lane < ROPE_DIM, cos, 1.0)
    s1_ref[...] = jnp.where(lane < ROPE_HALF, -sin, 0.0)
    s2_ref[...] = jnp.where((lane >= ROPE_HALF) & (lane < ROPE_DIM), sin, 0.0)


def rope_tables(positions, *, tm=512):
    M = positions.size
    inv_freq = 1.0 / (ROPE_THETA ** (jnp.arange(0, ROPE_DIM, 2, dtype=F32) / ROPE_DIM))
    invf = jnp.concatenate([inv_freq, inv_freq, jnp.zeros((LANES - ROPE_DIM,), F32)]).reshape(1, LANES)
    tab = jax.ShapeDtypeStruct((M, LANES), F32)
    row = pl.BlockSpec((tm, LANES), lambda i: (i, 0))
    return pl.pallas_call(
        _rope_table_kernel,
        grid=(M // tm,),
        in_specs=[pl.BlockSpec((tm, 1), lambda i: (i, 0)), pl.BlockSpec((1, LANES), lambda i: (0, 0))],
        out_specs=[row, row, row],
        out_shape=[tab, tab, tab],
        compiler_params=_params("parallel"),
        name="rope_tables",
    )(positions.reshape(M, 1), invf)


ROPE_ROW_CHUNK = 256


def _mm_rope_kernel(a_ref, w_ref, c_ref, s1_ref, s2_ref, o_ref, *stage_refs, tiles_per_part, dilation):
    part = pl.program_id(1) // tiles_per_part
    is_rope = part < 2
    mult = jnp.where(part == 0, Q_PRESCALE, 1.0)
    for r0 in range(0, a_ref.shape[0], ROPE_ROW_CHUNK):
        rows = slice(r0, r0 + ROPE_ROW_CHUNK)
        acc = jnp.dot(a_ref[rows, :], w_ref[...], preferred_element_type=F32)
        c = jnp.where(is_rope, c_ref[rows, :] * mult, 1.0)
        s1 = jnp.where(is_rope, s1_ref[rows, :] * mult, 0.0)
        s2 = jnp.where(is_rope, s2_ref[rows, :] * mult, 0.0)
        stage = stage_refs[(r0 // ROPE_ROW_CHUNK) % 2] if dilation > 1 else None
        for h in range(acc.shape[1] // HEAD_DIM):
            sl = slice(h * HEAD_DIM, (h + 1) * HEAD_DIM)
            t = acc[:, sl]
            fwd = pltpu.roll(t, HEAD_DIM - ROPE_HALF, 1)
            bwd = pltpu.roll(t, ROPE_HALF, 1)
            roped = t * c + fwd * s1 + bwd * s2
            if dilation == 1:
                o_ref[rows, sl] = roped.astype(o_ref.dtype)
            else:
                stage[h] = roped
        if dilation > 1:
            n = ROPE_ROW_CHUNK // dilation
            sup, i0 = divmod(r0 // dilation, o_ref.shape[2])
            for h in range(acc.shape[1] // HEAD_DIM):
                sl = slice(h * HEAD_DIM, (h + 1) * HEAD_DIM)
                for r in range(dilation):
                    o_ref[sup, r, i0:i0 + n, sl] = stage[h, pl.ds(r, n, stride=dilation), :].astype(o_ref.dtype)


def matmul_rope(a, w, widx, tabs, *, tm, tn, part_width, col0=0, col_stride=1, dilation=1):
    M, K = a.shape
    tiles_per_part = part_width // tn
    tab = pl.BlockSpec((tm, LANES), lambda i, j: (i, 0))

    def col(j):
        return col0 + (j // tiles_per_part) * col_stride * tiles_per_part + j % tiles_per_part

    if dilation == 1:
        out_spec = pl.BlockSpec((tm, tn), lambda i, j: (i, j))
        out_shape = jax.ShapeDtypeStruct((M, 3 * part_width), BF16)
        scratch = []
    else:
        span = BAND_BLOCK * dilation
        tiles_per_span = max(span // tm, 1)
        out_spec = pl.BlockSpec((max(tm // span, 1), dilation, min(BAND_BLOCK, tm // dilation), tn),
                                lambda i, j: (i // tiles_per_span, 0, i % tiles_per_span, j))
        out_shape = jax.ShapeDtypeStruct((M // span, dilation, BAND_BLOCK, 3 * part_width), BF16)
        scratch = [pltpu.VMEM((tn // HEAD_DIM, ROPE_ROW_CHUNK, HEAD_DIM), F32)] * 2
    return pl.pallas_call(
        functools.partial(_mm_rope_kernel, tiles_per_part=tiles_per_part, dilation=dilation),
        grid=(M // tm, 3 * tiles_per_part),
        in_specs=[pl.BlockSpec((tm, K), lambda i, j: (i, 0)), _weight_spec(w, widx, tn, col), tab, tab, tab],
        out_specs=out_spec,
        out_shape=out_shape,
        scratch_shapes=scratch,
        compiler_params=_params("parallel", "arbitrary"),
        name="matmul_rope",
    )(a, w, *tabs)


_NT = (((1,), (1,)), ((), ()))


DIL_BLOCKS_PER_STEP = 2


def _dil_attn_kernel(q_ref, kp_ref, kc_ref, vp_ref, vc_ref, o_ref, lse_ref):
    qi = lax.broadcasted_iota(jnp.int32, (BAND_BLOCK, 2 * BAND_BLOCK), 0)
    kj = lax.broadcasted_iota(jnp.int32, (BAND_BLOCK, 2 * BAND_BLOCK), 1)
    band = (kj >= qi) & (kj <= qi + BAND_BLOCK)
    lane = lax.broadcasted_iota(jnp.int32, (BAND_BLOCK, LANES), 1)
    for u in range(DIL_BLOCKS_PER_STEP):
        rows = slice(u * BAND_BLOCK, (u + 1) * BAND_BLOCK)
        if u == 0:
            mask = band & ((pl.program_id(2) > 0) | (kj >= BAND_BLOCK))
        else:
            mask = band
        lse_tile = jnp.zeros((BAND_BLOCK, LANES), F32)
        for j in range(DIL_HEADS_PER_GROUP):
            sl = slice(j * HEAD_DIM, (j + 1) * HEAD_DIM)
            k_prev = kp_ref[:, sl] if u == 0 else kc_ref[u - 1, :, sl]
            v_prev = vp_ref[:, sl] if u == 0 else vc_ref[u - 1, :, sl]
            k = jnp.concatenate([k_prev, kc_ref[u, :, sl]], axis=0)
            v = jnp.concatenate([v_prev, vc_ref[u, :, sl]], axis=0)
            s = lax.dot_general(q_ref[u, :, sl], k, _NT, preferred_element_type=F32)
            s = jnp.where(mask, s, -jnp.inf)
            m = jnp.max(s, axis=-1, keepdims=True)
            p = jnp.exp2(s - m)
            den = jnp.sum(p, axis=-1, keepdims=True)
            o = jnp.dot(p.astype(BF16), v, preferred_element_type=F32)
            o_ref[0, rows, sl] = o * (1.0 / den)
            lse_tile = jnp.where(lane == j, m * LN2 + jnp.log(den), lse_tile)
        lse_ref[0, rows, :] = lse_tile


def dilated_group_attention(qkv, group, dilation, batch):
    M = qkv.shape[0] * dilation * BAND_BLOCK
    S = M // batch
    L = S // dilation
    nb = L // BAND_BLOCK
    per = DIL_BLOCKS_PER_STEP
    assert nb % per == 0, (nb, per)
    steps = nb // per

    def spec(part, prev):
        if prev:
            return pl.BlockSpec((None, None, BAND_BLOCK, DIL_GROUP_WIDTH),
                                lambda b, r, n: (b * nb + jnp.maximum(n * per - 1, 0), r, 0, part))
        return pl.BlockSpec((per, None, BAND_BLOCK, DIL_GROUP_WIDTH),
                            lambda b, r, n: (b * steps + n, r, 0, part))

    o, lse = pl.pallas_call(
        _dil_attn_kernel,
        grid=(batch, dilation, steps),
        in_specs=[spec(0, False), spec(1, True), spec(1, False), spec(2, True), spec(2, False)],
        out_specs=[pl.BlockSpec((1, per * BAND_BLOCK, DIL_GROUP_WIDTH), lambda b, r, n: (b, n, r)),
                   pl.BlockSpec((1, per * BAND_BLOCK, LANES), lambda b, r, n: (b, n, r))],
        out_shape=[jax.ShapeDtypeStruct((batch, L, dilation * DIL_GROUP_WIDTH), F32),
                   jax.ShapeDtypeStruct((batch, L, dilation * LANES), F32)],
        compiler_params=_params("parallel", "parallel", "arbitrary"),
        name=f"dil_attn_g{group}",
    )(qkv, qkv, qkv, qkv, qkv)
    return o.reshape(M, DIL_GROUP_WIDTH), lse.reshape(M, LANES)


def _dil_mix_kernel(o0_ref, o1_ref, o2_ref, l0_ref, l1_ref, l2_ref, out_ref):
    lses = [l0_ref[...], l1_ref[...], l2_ref[...]]
    mx = jnp.maximum(jnp.maximum(lses[0], lses[1]), lses[2])
    es = [jnp.exp(l - mx) for l in lses]
    tot = es[0] + es[1] + es[2]
    for g, o_ref in enumerate((o0_ref, o1_ref, o2_ref)):
        alpha = es[g] / tot
        for j in range(DIL_HEADS_PER_GROUP):
            src = slice(j * HEAD_DIM, (j + 1) * HEAD_DIM)
            col = (g * DIL_HEADS_PER_GROUP + j) * HEAD_DIM
            out_ref[:, col:col + HEAD_DIM] = (alpha[:, j:j + 1] * o_ref[:, src]).astype(out_ref.dtype)


def dilated_mix(outs, lses, *, tm=256):
    M = outs[0].shape[0]
    o_spec = pl.BlockSpec((tm, DIL_GROUP_WIDTH), lambda i: (i, 0))
    l_spec = pl.BlockSpec((tm, LANES), lambda i: (i, 0))
    return pl.pallas_call(
        _dil_mix_kernel,
        grid=(M // tm,),
        in_specs=[o_spec] * 3 + [l_spec] * 3,
        out_specs=pl.BlockSpec((tm, DIL_WIDTH), lambda i: (i, 0)),
        out_shape=jax.ShapeDtypeStruct((M, DIL_WIDTH), BF16),
        compiler_params=_params("parallel"),
        name="dil_mix",
    )(*outs, *lses)


ONES_ROWS = 16


def _diff_attn_kernel(lam_ref, q_ref, k_ref, v_ref, g_ref, o_ref, qbd_sc, vT_sc, s0_sc, s1_sc, p0_sc, p1_sc,
                      bm0_sc, bm1_sc, a0_sc, a1_sc, m_sc, acc_sc, *, blk, kblk, lam_init):
    s_sc = (s0_sc, s1_sc)
    p_sc = (p0_sc, p1_sc)
    bm_sc = (bm0_sc, bm1_sc)
    a_sc = (a0_sc, a1_sc)
    seq = v_ref.shape[1]

    hw = v_ref.shape[2]
    for kb in range(seq // kblk):
        vT_sc[kb, 0:hw, :] = v_ref[0, kb * kblk:(kb + 1) * kblk, :].astype(F32).T.astype(BF16)
        vT_sc[kb, hw:hw + ONES_ROWS, :] = jnp.ones((ONES_ROWS, kblk), BF16)
    qbd_sc[...] = jnp.zeros(qbd_sc.shape, BF16)

    def query_block(qi, carry):
        _diff_attn_query_block(qi, lam_ref, q_ref, k_ref, g_ref, o_ref, qbd_sc, vT_sc, s_sc, p_sc, bm_sc, a_sc,
                               m_sc, acc_sc, blk=blk, kblk=kblk, lam_init=lam_init)
        return carry

    lax.fori_loop(0, seq // blk, query_block, 0)


def _diff_attn_query_block(qi, lam_ref, q_ref, k_ref, g_ref, o_ref, qbd_sc, vT_sc, s_sc, p_sc, bm_sc, a_sc,
                           m_sc, acc_sc, *, blk, kblk, lam_init):
    n_diag = blk // kblk
    n_blocks = n_diag * (qi + 1)
    q_rows = pl.ds(pl.multiple_of(qi * blk, blk), blk)

    qT = q_ref[0, q_rows, :].astype(F32).T.astype(BF16)
    qbd_sc[0:HEAD_DIM, 0:blk] = qT[0:HEAD_DIM]
    qbd_sc[HEAD_DIM:2 * HEAD_DIM, blk:2 * blk] = qT[HEAD_DIM:2 * HEAD_DIM]
    m_sc[...] = jnp.full(m_sc.shape, -jnp.inf, F32)
    acc_sc[...] = jnp.zeros(acc_sc.shape, F32)

    def key_block(t):
        return jnp.where(t < n_diag, n_diag * qi + t, t - n_diag)

    def scores(t, slot):
        k0 = pl.multiple_of(key_block(t) * kblk, kblk)
        sT = jnp.dot(k_ref[0, pl.ds(k0, kblk), :], qbd_sc[...], preferred_element_type=F32)
        if isinstance(t, int) and t < n_diag:
            key = t * kblk + lax.broadcasted_iota(jnp.int32, sT.shape, 0)
            qry = lax.broadcasted_iota(jnp.int32, sT.shape, 1) & (blk - 1)
            sT = jnp.where(key <= qry, sT, -jnp.inf)
        s_sc[slot][...] = sT
        bm_sc[slot][...] = jnp.max(sT, axis=0, keepdims=True)

    def softmax(slot):
        sT = s_sc[slot][...]
        m_prev = m_sc[...]
        m_new = jnp.maximum(m_prev, bm_sc[slot][...])
        alpha = jnp.exp2(m_prev - m_new)
        pT = jnp.exp2(sT - m_new)
        p_sc[slot][...] = pT.astype(BF16)
        a_sc[slot][...] = alpha
        m_sc[...] = m_new

    def accumulate(t, slot):
        pv = jnp.dot(vT_sc[key_block(t)], p_sc[slot][...], preferred_element_type=F32)
        acc_sc[...] = a_sc[slot][...] * acc_sc[...] + pv

    assert n_diag in (1, 2)
    scores(0, 0)

    @pl.when(n_blocks == 1)
    def _():
        softmax(0)
        accumulate(0, 0)

    @pl.when(n_blocks >= 2)
    def _():
        scores(1, 1)
        softmax(0)
        n_steady = n_blocks - 2

        def tick_pair(i, carry):
            t = 1 + 2 * i
            softmax(1)
            accumulate(t - 1, 0)
            scores(t + 1, 0)
            softmax(0)
            accumulate(t, 1)
            scores(t + 2, 1)
            return carry

        lax.fori_loop(0, n_steady // 2, tick_pair, 0)
        last = n_blocks - 1

        @pl.when(n_steady % 2 == 1)
        def _():
            accumulate(last - 2, 0)
            scores(last, 0)
            softmax(1)
            accumulate(last - 1, 1)
            softmax(0)
            accumulate(last, 0)

        @pl.when(n_steady % 2 == 0)
        def _():
            accumulate(last - 1, 0)
            softmax(1)
            accumulate(last, 1)

    lp = lam_ref[...]
    lam = (jnp.exp(jnp.sum(lp[0:1] * lp[1:2], axis=-1, keepdims=True))
           - jnp.exp(jnp.sum(lp[2:3] * lp[3:4], axis=-1, keepdims=True)) + lam_init)
    hw = 2 * HEAD_DIM
    o_all = acc_sc[0:hw, :] * (1.0 / acc_sc[hw:hw + 1, :])
    oT = o_all[:, 0:blk] - lam * o_all[:, blk:2 * blk]
    inv_rms = lax.rsqrt(jnp.mean(oT * oT, axis=0, keepdims=True) + SUBLN_EPS)
    oT = oT * inv_rms * (g_ref[...] * (1.0 - lam_init))
    o_ref[0, q_rows, :] = oT.T.astype(o_ref.dtype)


def diff_attention(qkv, lam_params, subln_g, batch, layer_idx, *, blk=512, kblk=512):
    M, W3 = qkv.shape
    W = W3 // 3
    S = M // batch
    heads = W // (2 * HEAD_DIM)
    hw = 2 * HEAD_DIM
    lam_init = 0.8 - 0.6 * float(np.exp(-0.3 * layer_idx))
    view = qkv.reshape(batch, S, W3)
    out = pl.pallas_call(
        functools.partial(_diff_attn_kernel, blk=blk, kblk=kblk, lam_init=lam_init),
        grid=(batch, heads),
        in_specs=[
            pl.BlockSpec((4, HEAD_DIM), lambda b, h: (0, 0)),
            pl.BlockSpec((1, S, hw), lambda b, h: (b, 0, h)),
            pl.BlockSpec((1, S, hw), lambda b, h: (b, 0, heads + h)),
            pl.BlockSpec((1, S, hw), lambda b, h: (b, 0, 2 * heads + h)),
            pl.BlockSpec((hw, 1), lambda b, h: (0, 0)),
        ],
        out_specs=pl.BlockSpec((1, S, hw), lambda b, h: (b, 0, h)),
        out_shape=jax.ShapeDtypeStruct((batch, S, W), BF16),
        scratch_shapes=[
            pltpu.VMEM((hw, 2 * blk), BF16),
            pltpu.VMEM((S // kblk, hw + ONES_ROWS, kblk), BF16),
            pltpu.VMEM((kblk, 2 * blk), F32), pltpu.VMEM((kblk, 2 * blk), F32),
            pltpu.VMEM((kblk, 2 * blk), BF16), pltpu.VMEM((kblk, 2 * blk), BF16),
            pltpu.VMEM((1, 2 * blk), F32), pltpu.VMEM((1, 2 * blk), F32),
            pltpu.VMEM((1, 2 * blk), F32), pltpu.VMEM((1, 2 * blk), F32),
            pltpu.VMEM((1, 2 * blk), F32),
            pltpu.VMEM((hw + ONES_ROWS, 2 * blk), F32),
        ],
        compiler_params=_params("parallel", "parallel"),
        name="diff_attn",
    )(lam_params, view, view, view, subln_g.reshape(hw, 1))
    return out.reshape(M, W)


def _ffn(x, h, w_in, w_out, widx, g_post, g_next):
    act = matmul_swiglu(h, w_in, widx, tm=1024, tn=512)
    return matmul_resid(act, w_out, widx, x, g_post, g_next, 0.5, tm=1024, tk=512)


def kernel(x, positions, norm_g, ffn_in, ffn_out, dil_w_in, dil_w_out, diff_w_in, diff_lambda,
           diff_subln_g, diff_w_out):
    B, S, D = x.shape
    depth = norm_g.shape[0]
    M = B * S
    x = x.reshape(M, D)
    g = norm_g.reshape(depth * 6, 1, D)
    ffn_in, ffn_out, dil_w_in, dil_w_out, diff_w_in, diff_w_out = (
        w.astype(BF16) for w in (ffn_in, ffn_out, dil_w_in, dil_w_out, diff_w_in, diff_w_out))
    tabs = rope_tables(positions)

    h = rms_cast(x, g[0])
    for l in range(depth):
        gl = g[6 * l:6 * l + 6]
        x, h = _ffn(x, h, ffn_in, ffn_out, (l, 0), gl[1], gl[2])
        if l % 2 == 0:
            parts = []
            for grp, (_, dil) in enumerate(DIL_PATTERNS):
                qkv = matmul_rope(h, dil_w_in, (l // 2,), tabs, tm=1024, tn=DIL_GROUP_WIDTH,
                                  part_width=DIL_GROUP_WIDTH, col0=grp, col_stride=DIL_GROUPS, dilation=dil)
                qkv = qkv.reshape(M // (BAND_BLOCK * dil), dil, BAND_BLOCK, 3 * DIL_GROUP_WIDTH)
                parts.append(dilated_group_attention(qkv, grp, dil, B))
            mixed = dilated_mix([p[0] for p in parts], [p[1] for p in parts])
            x, h = matmul_resid(mixed, dil_w_out, (l // 2,), x, gl[3], gl[4], 1.0, tm=1024, tk=768)
        else:
            qkv = matmul_rope(h, diff_w_in, (l // 2,), tabs, tm=1024, tn=1024,
                              part_width=diff_w_in.shape[-1] // 3)
            att = diff_attention(qkv, diff_lambda[l // 2], diff_subln_g[l // 2], B, l)
            x, h = matmul_resid(att, diff_w_out, (l // 2,), x, gl[3], gl[4], 1.0, tm=1024, tk=512)
        g_next = g[6 * l + 6] if l + 1 < depth else None
        x, h = _ffn(x, h, ffn_in, ffn_out, (l, 1), gl[5], g_next)
    return x.reshape(B, S, D)
```

```python
import functools

import numpy as np
import jax
import jax.numpy as jnp
from jax import lax
from jax.experimental import pallas as pl
from jax.experimental.pallas import tpu as pltpu

HEAD_DIM = 128
ROPE_THETA = 500000.0
ROPE_DIM = HEAD_DIM // 4
ROPE_HALF = ROPE_DIM // 2
NORM_EPS = 1e-6
SUBLN_EPS = 1e-5
DIL_PATTERNS = ((128, 1), (512, 4), (2048, 16))
DIL_GROUPS = len(DIL_PATTERNS)
DIL_HEADS_PER_GROUP = 10
DIL_HEADS = DIL_GROUPS * DIL_HEADS_PER_GROUP
DIL_WIDTH = DIL_HEADS * HEAD_DIM
DIL_GROUP_WIDTH = DIL_HEADS_PER_GROUP * HEAD_DIM
BAND_BLOCK = 128
LANES = 128
LN2 = float(np.log(2.0))
Q_PRESCALE = float(HEAD_DIM ** -0.5 / np.log(2.0))
assert all(window // dil == BAND_BLOCK for window, dil in DIL_PATTERNS)

VMEM_LIMIT_BYTES = 56 * 1024 * 1024

F32 = jnp.float32
BF16 = jnp.bfloat16


def _params(*sem):
    return pltpu.CompilerParams(dimension_semantics=sem, vmem_limit_bytes=VMEM_LIMIT_BYTES)


def _rms(x, eps):
    return x * lax.rsqrt(jnp.mean(x * x, axis=-1, keepdims=True) + eps)


def _inv_rms(x):
    return lax.rsqrt(jnp.mean(x * x, axis=-1, keepdims=True) + NORM_EPS)


def _rms_cast_kernel(x_ref, g_ref, o_ref):
    o_ref[...] = (_rms(x_ref[...], NORM_EPS) * g_ref[...]).astype(o_ref.dtype)


def rms_cast(x, g, *, tm=256):
    M, D = x.shape
    return pl.pallas_call(
        _rms_cast_kernel,
        grid=(M // tm,),
        in_specs=[pl.BlockSpec((tm, D), lambda i: (i, 0)), pl.BlockSpec((1, D), lambda i: (0, 0))],
        out_specs=pl.BlockSpec((tm, D), lambda i: (i, 0)),
        out_shape=jax.ShapeDtypeStruct((M, D), BF16),
        compiler_params=_params("parallel"),
        name="rms_cast",
    )(x, g)


def _weight_spec(w, widx, tn, col=lambda j: j):
    K = w.shape[-2]
    return pl.BlockSpec((None,) * len(widx) + (K, tn), lambda i, j: (*widx, 0, col(j)))


RESID_COL_CHUNK = 1024
RESID_ROW_CHUNK = 128


def _mm_resid_kernel(a_ref, w_ref, x_hbm, gp_ref, gn_ref, xo_hbm, *rest, coef, with_h):
    if with_h:
        ho_hbm, acc_sc, x_sc, h_sc, x_sem, xo_sem, h_sem = rest
    else:
        acc_sc, x_sc, x_sem, xo_sem = rest
    k = pl.program_id(1)
    tm = acc_sc.shape[0]
    row0 = pl.multiple_of(pl.program_id(0) * tm, tm)
    x_copy = pltpu.make_async_copy(x_hbm.at[pl.ds(row0, tm)], x_sc, x_sem)

    def partial_products():
        a = a_ref[...]
        for n0 in range(0, acc_sc.shape[1], RESID_COL_CHUNK):
            cols = slice(n0, n0 + RESID_COL_CHUNK)
            yield cols, jnp.dot(a, w_ref[:, cols], preferred_element_type=F32)

    @pl.when(k == 0)
    def _():
        x_copy.start()
        for cols, part in partial_products():
            acc_sc[:, cols] = part

    @pl.when(k > 0)
    def _():
        for cols, part in partial_products():
            acc_sc[:, cols] += part

    @pl.when(k == pl.num_programs(1) - 1)
    def _():
        x_copy.wait()
        gain = coef * gp_ref[...]
        xo_copies, h_copies = [], []
        for c in range(tm // RESID_ROW_CHUNK):
            rows = slice(c * RESID_ROW_CHUNK, (c + 1) * RESID_ROW_CHUNK)
            out_rows = pl.ds(row0 + c * RESID_ROW_CHUNK, RESID_ROW_CHUNK)
            acc_sc[rows, :] = x_sc[rows, :] + acc_sc[rows, :] * _inv_rms(acc_sc[rows, :]) * gain
            xo_copies.append(pltpu.make_async_copy(acc_sc.at[rows], xo_hbm.at[out_rows], xo_sem.at[c]))
            xo_copies[-1].start()
            if with_h:
                slot = c % 2
                if c >= 2:
                    h_copies[c - 2].wait()
                h_sc[slot] = (acc_sc[rows, :] * _inv_rms(acc_sc[rows, :]) * gn_ref[...]).astype(h_sc.dtype)
                h_copies.append(pltpu.make_async_copy(h_sc.at[slot], ho_hbm.at[out_rows], h_sem.at[slot]))
                h_copies[-1].start()
        for cp in h_copies[-2:] + xo_copies:
            cp.wait()


def matmul_resid(a, w, widx, x, g_post, g_next, coef, *, tm, tk):
    M, K = a.shape
    D = w.shape[-1]
    vec = pl.BlockSpec((1, D), lambda i, k: (0, 0))
    hbm = pl.BlockSpec(memory_space=pl.ANY)
    with_h = g_next is not None
    n_chunks = tm // RESID_ROW_CHUNK
    scratch = [pltpu.VMEM((tm, D), F32), pltpu.VMEM((tm, D), F32)]
    sems = [pltpu.SemaphoreType.DMA(()), pltpu.SemaphoreType.DMA((n_chunks,))]
    if with_h:
        scratch.append(pltpu.VMEM((2, RESID_ROW_CHUNK, D), BF16))
        sems.append(pltpu.SemaphoreType.DMA((2,)))
    out = pl.pallas_call(
        functools.partial(_mm_resid_kernel, coef=coef, with_h=with_h),
        grid=(M // tm, K // tk),
        in_specs=[pl.BlockSpec((tm, tk), lambda i, k: (i, k)),
                  pl.BlockSpec((None,) * len(widx) + (tk, D), lambda i, k: (*widx, k, 0)),
                  hbm, vec, vec],
        out_specs=[hbm, hbm] if with_h else [hbm],
        out_shape=[jax.ShapeDtypeStruct((M, D), F32)] + ([jax.ShapeDtypeStruct((M, D), BF16)] if with_h else []),
        scratch_shapes=scratch + sems,
        compiler_params=_params("arbitrary", "arbitrary"),
        name="matmul_resid",
    )(a, w, x, g_post, g_next if with_h else g_post)
    return (out[0], out[1]) if with_h else (out[0], None)


def _mm_swiglu_kernel(a_ref, wg_ref, wu_ref, o_ref):
    a = a_ref[...]
    gate = jnp.dot(a, wg_ref[...], preferred_element_type=F32)
    up = jnp.dot(a, wu_ref[...], preferred_element_type=F32)
    o_ref[...] = (gate * jax.nn.sigmoid(gate) * up).astype(o_ref.dtype)


def matmul_swiglu(a, w_in, widx, *, tm, tn):
    M, K = a.shape
    F = w_in.shape[-1] // 2
    nj = F // tn
    return pl.pallas_call(
        _mm_swiglu_kernel,
        grid=(M // tm, nj),
        in_specs=[
            pl.BlockSpec((tm, K), lambda i, j: (i, 0)),
            _weight_spec(w_in, widx, tn),
            _weight_spec(w_in, widx, tn, lambda j: j + nj),
        ],
        out_specs=pl.BlockSpec((tm, tn), lambda i, j: (i, j)),
        out_shape=jax.ShapeDtypeStruct((M, F), BF16),
        compiler_params=_params("parallel", "arbitrary"),
        name="matmul_swiglu",
    )(a, w_in, w_in)


def _rope_table_kernel(pos_ref, invf_ref, c_ref, s1_ref, s2_ref):
    ang = pos_ref[...].astype(F32) * invf_ref[...]
    lane = lax.broadcasted_iota(jnp.int32, ang.shape, 1)
    cos = jnp.cos(ang)
    sin = jnp.sin(ang)
    c_ref[...] = jnp.where(lane < ROPE_DIM, cos, 1.0)
    s1_ref[...] = jnp.where(lane < ROPE_HALF, -sin, 0.0)
    s2_ref[...] = jnp.where((lane >= ROPE_HALF) & (lane < ROPE_DIM), sin, 0.0)


def rope_tables(positions, *, tm=512):
    M = positions.size
    inv_freq = 1.0 / (ROPE_THETA ** (jnp.arange(0, ROPE_DIM, 2, dtype=F32) / ROPE_DIM))
    invf = jnp.concatenate([inv_freq, inv_freq, jnp.zeros((LANES - ROPE_DIM,), F32)]).reshape(1, LANES)
    tab = jax.ShapeDtypeStruct((M, LANES), F32)
    row = pl.BlockSpec((tm, LANES), lambda i: (i, 0))
    return pl.pallas_call(
        _rope_table_kernel,
        grid=(M // tm,),
        in_specs=[pl.BlockSpec((tm, 1), lambda i: (i, 0)), pl.BlockSpec((1, LANES), lambda i: (0, 0))],
        out_specs=[row, row, row],
        out_shape=[tab, tab, tab],
        compiler_params=_params("parallel"),
        name="rope_tables",
    )(positions.reshape(M, 1), invf)


ROPE_ROW_CHUNK = 256


def _mm_rope_kernel(a_ref, w_ref, c_ref, s1_ref, s2_ref, o_ref, *stage_refs, tiles_per_part, dilation):
    part = pl.program_id(1) // tiles_per_part
    is_rope = part < 2
    mult = jnp.where(part == 0, Q_PRESCALE, 1.0)
    for r0 in range(0, a_ref.shape[0], ROPE_ROW_CHUNK):
        rows = slice(r0, r0 + ROPE_ROW_CHUNK)
        acc = jnp.dot(a_ref[rows, :], w_ref[...], preferred_element_type=F32)
        c = jnp.where(is_rope, c_ref[rows, :] * mult, 1.0)
        s1 = jnp.where(is_rope, s1_ref[rows, :] * mult, 0.0)
        s2 = jnp.where(is_rope, s2_ref[rows, :] * mult, 0.0)
        stage = stage_refs[(r0 // ROPE_ROW_CHUNK) % 2] if dilation > 1 else None
        for h in range(acc.shape[1] // HEAD_DIM):
            sl = slice(h * HEAD_DIM, (h + 1) * HEAD_DIM)
            t = acc[:, sl]
            fwd = pltpu.roll(t, HEAD_DIM - ROPE_HALF, 1)
            bwd = pltpu.roll(t, ROPE_HALF, 1)
            roped = t * c + fwd * s1 + bwd * s2
            if dilation == 1:
                o_ref[rows, sl] = roped.astype(o_ref.dtype)
            else:
                stage[h] = roped
        if dilation > 1:
            n = ROPE_ROW_CHUNK // dilation
            sup, i0 = divmod(r0 // dilation, o_ref.shape[2])
            for h in range(acc.shape[1] // HEAD_DIM):
                sl = slice(h * HEAD_DIM, (h + 1) * HEAD_DIM)
                for r in range(dilation):
                    o_ref[sup, r, i0:i0 + n, sl] = stage[h, pl.ds(r, n, stride=dilation), :].astype(o_ref.dtype)


def matmul_rope(a, w, widx, tabs, *, tm, tn, part_width, col0=0, col_stride=1, dilation=1):
    M, K = a.shape
    tiles_per_part = part_width // tn
    tab = pl.BlockSpec((tm, LANES), lambda i, j: (i, 0))

    def col(j):
        return col0 + (j // tiles_per_part) * col_stride * tiles_per_part + j % tiles_per_part

    if dilation == 1:
        out_spec = pl.BlockSpec((tm, tn), lambda i, j: (i, j))
        out_shape = jax.ShapeDtypeStruct((M, 3 * part_width), BF16)
        scratch = []
    else:
        span = BAND_BLOCK * dilation
        tiles_per_span = max(span // tm, 1)
        out_spec = pl.BlockSpec((max(tm // span, 1), dilation, min(BAND_BLOCK, tm // dilation), tn),
                                lambda i, j: (i // tiles_per_span, 0, i % tiles_per_span, j))
        out_shape = jax.ShapeDtypeStruct((M // span, dilation, BAND_BLOCK, 3 * part_width), BF16)
        scratch = [pltpu.VMEM((tn // HEAD_DIM, ROPE_ROW_CHUNK, HEAD_DIM), F32)] * 2
    return pl.pallas_call(
        functools.partial(_mm_rope_kernel, tiles_per_part=tiles_per_part, dilation=dilation),
        grid=(M // tm, 3 * tiles_per_part),
        in_specs=[pl.BlockSpec((tm, K), lambda i, j: (i, 0)), _weight_spec(w, widx, tn, col), tab, tab, tab],
        out_specs=out_spec,
        out_shape=out_shape,
        scratch_shapes=scratch,
        compiler_params=_params("parallel", "arbitrary"),
        name="matmul_rope",
    )(a, w, *tabs)


_NT = (((1,), (1,)), ((), ()))


DIL_BLOCKS_PER_STEP = 4


def _dil_attn_kernel(q_ref, kp_ref, kc_ref, vp_ref, vc_ref, o_ref, lse_ref):
    qi = lax.broadcasted_iota(jnp.int32, (BAND_BLOCK, 2 * BAND_BLOCK), 0)
    kj = lax.broadcasted_iota(jnp.int32, (BAND_BLOCK, 2 * BAND_BLOCK), 1)
    band = (kj >= qi) & (kj <= qi + BAND_BLOCK)
    lane = lax.broadcasted_iota(jnp.int32, (BAND_BLOCK, LANES), 1)
    for u in range(DIL_BLOCKS_PER_STEP):
        rows = slice(u * BAND_BLOCK, (u + 1) * BAND_BLOCK)
        if u == 0:
            mask = band & ((pl.program_id(2) > 0) | (kj >= BAND_BLOCK))
        else:
            mask = band
        lse_tile = jnp.zeros((BAND_BLOCK, LANES), F32)
        for j in range(DIL_HEADS_PER_GROUP):
            sl = slice(j * HEAD_DIM, (j + 1) * HEAD_DIM)
            k_prev = kp_ref[:, sl] if u == 0 else kc_ref[u - 1, :, sl]
            v_prev = vp_ref[:, sl] if u == 0 else vc_ref[u - 1, :, sl]
            k = jnp.concatenate([k_prev, kc_ref[u, :, sl]], axis=0)
            v = jnp.concatenate([v_prev, vc_ref[u, :, sl]], axis=0)
            s = lax.dot_general(q_ref[u, :, sl], k, _NT, preferred_element_type=F32)
            s = jnp.where(mask, s, -jnp.inf)
            m = jnp.max(s, axis=-1, keepdims=True)
            p = jnp.exp2(s - m)
            den = jnp.sum(p, axis=-1, keepdims=True)
            o = jnp.dot(p.astype(BF16), v, preferred_element_type=F32)
            o_ref[0, rows, sl] = o * (1.0 / den)
            lse_tile = jnp.where(lane == j, m * LN2 + jnp.log(den), lse_tile)
        lse_ref[0, rows, :] = lse_tile


def dilated_group_attention(qkv, group, dilation, batch):
    M = qkv.shape[0] * dilation * BAND_BLOCK
    S = M // batch
    L = S // dilation
    nb = L // BAND_BLOCK
    per = DIL_BLOCKS_PER_STEP
    assert nb % per == 0, (nb, per)
    steps = nb // per

    def spec(part, prev):
        if prev:
            return pl.BlockSpec((None, None, BAND_BLOCK, DIL_GROUP_WIDTH),
                                lambda b, r, n: (b * nb + jnp.maximum(n * per - 1, 0), r, 0, part))
        return pl.BlockSpec((per, None, BAND_BLOCK, DIL_GROUP_WIDTH),
                            lambda b, r, n: (b * steps + n, r, 0, part))

    o, lse = pl.pallas_call(
        _dil_attn_kernel,
        grid=(batch, dilation, steps),
        in_specs=[spec(0, False), spec(1, True), spec(1, False), spec(2, True), spec(2, False)],
        out_specs=[pl.BlockSpec((1, per * BAND_BLOCK, DIL_GROUP_WIDTH), lambda b, r, n: (b, n, r)),
                   pl.BlockSpec((1, per * BAND_BLOCK, LANES), lambda b, r, n: (b, n, r))],
        out_shape=[jax.ShapeDtypeStruct((batch, L, dilation * DIL_GROUP_WIDTH), F32),
                   jax.ShapeDtypeStruct((batch, L, dilation * LANES), F32)],
        compiler_params=_params("parallel", "parallel", "arbitrary"),
        name=f"dil_attn_g{group}",
    )(qkv, qkv, qkv, qkv, qkv)
    return o.reshape(M, DIL_GROUP_WIDTH), lse.reshape(M, LANES)


def _dil_mix_kernel(o0_ref, o1_ref, o2_ref, l0_ref, l1_ref, l2_ref, out_ref):
    lses = [l0_ref[...], l1_ref[...], l2_ref[...]]
    mx = jnp.maximum(jnp.maximum(lses[0], lses[1]), lses[2])
    es = [jnp.exp(l - mx) for l in lses]
    tot = es[0] + es[1] + es[2]
    for g, o_ref in enumerate((o0_ref, o1_ref, o2_ref)):
        alpha = es[g] / tot
        for j in range(DIL_HEADS_PER_GROUP):
            src = slice(j * HEAD_DIM, (j + 1) * HEAD_DIM)
            col = (g * DIL_HEADS_PER_GROUP + j) * HEAD_DIM
            out_ref[:, col:col + HEAD_DIM] = (alpha[:, j:j + 1] * o_ref[:, src]).astype(out_ref.dtype)


def dilated_mix(outs, lses, *, tm=256):
    M = outs[0].shape[0]
    o_spec = pl.BlockSpec((tm, DIL_GROUP_WIDTH), lambda i: (i, 0))
    l_spec = pl.BlockSpec((tm, LANES), lambda i: (i, 0))
    return pl.pallas_call(
        _dil_mix_kernel,
        grid=(M // tm,),
        in_specs=[o_spec] * 3 + [l_spec] * 3,
        out_specs=pl.BlockSpec((tm, DIL_WIDTH), lambda i: (i, 0)),
        out_shape=jax.ShapeDtypeStruct((M, DIL_WIDTH), BF16),
        compiler_params=_params("parallel"),
        name="dil_mix",
    )(*outs, *lses)


ONES_ROWS = 16


def _diff_attn_kernel(lam_ref, q_ref, k_ref, v_ref, g_ref, o_ref, qbd_sc, vT_sc, s0_sc, s1_sc, p0_sc, p1_sc,
                      bm0_sc, bm1_sc, a0_sc, a1_sc, m_sc, acc_sc, *, blk, kblk, lam_init):
    s_sc = (s0_sc, s1_sc)
    p_sc = (p0_sc, p1_sc)
    bm_sc = (bm0_sc, bm1_sc)
    a_sc = (a0_sc, a1_sc)
    seq = v_ref.shape[1]

    hw = v_ref.shape[2]
    for kb in range(seq // kblk):
        vT_sc[kb, 0:hw, :] = v_ref[0, kb * kblk:(kb + 1) * kblk, :].astype(F32).T.astype(BF16)
        vT_sc[kb, hw:hw + ONES_ROWS, :] = jnp.ones((ONES_ROWS, kblk), BF16)
    qbd_sc[...] = jnp.zeros(qbd_sc.shape, BF16)

    def query_block(qi, carry):
        _diff_attn_query_block(qi, lam_ref, q_ref, k_ref, g_ref, o_ref, qbd_sc, vT_sc, s_sc, p_sc, bm_sc, a_sc,
                               m_sc, acc_sc, blk=blk, kblk=kblk, lam_init=lam_init)
        return carry

    lax.fori_loop(0, seq // blk, query_block, 0)


def _diff_attn_query_block(qi, lam_ref, q_ref, k_ref, g_ref, o_ref, qbd_sc, vT_sc, s_sc, p_sc, bm_sc, a_sc,
                           m_sc, acc_sc, *, blk, kblk, lam_init):
    n_diag = blk // kblk
    n_blocks = n_diag * (qi + 1)
    q_rows = pl.ds(pl.multiple_of(qi * blk, blk), blk)

    qT = q_ref[0, q_rows, :].astype(F32).T.astype(BF16)
    qbd_sc[0:HEAD_DIM, 0:blk] = qT[0:HEAD_DIM]
    qbd_sc[HEAD_DIM:2 * HEAD_DIM, blk:2 * blk] = qT[HEAD_DIM:2 * HEAD_DIM]
    m_sc[...] = jnp.full(m_sc.shape, -jnp.inf, F32)
    acc_sc[...] = jnp.zeros(acc_sc.shape, F32)

    def key_block(t):
        return jnp.where(t < n_diag, n_diag * qi + t, t - n_diag)

    def scores(t, slot):
        k0 = pl.multiple_of(key_block(t) * kblk, kblk)
        sT = jnp.dot(k_ref[0, pl.ds(k0, kblk), :], qbd_sc[...], preferred_element_type=F32)
        if isinstance(t, int) and t < n_diag:
            key = t * kblk + lax.broadcasted_iota(jnp.int32, sT.shape, 0)
            qry = lax.broadcasted_iota(jnp.int32, sT.shape, 1) & (blk - 1)
            sT = jnp.where(key <= qry, sT, -jnp.inf)
        s_sc[slot][...] = sT
        bm_sc[slot][...] = jnp.max(sT, axis=0, keepdims=True)

    def softmax(slot):
        sT = s_sc[slot][...]
        m_prev = m_sc[...]
        m_new = jnp.maximum(m_prev, bm_sc[slot][...])
        alpha = jnp.exp2(m_prev - m_new)
        pT = jnp.exp2(sT - m_new)
        p_sc[slot][...] = pT.astype(BF16)
        a_sc[slot][...] = alpha
        m_sc[...] = m_new

    def accumulate(t, slot):
        pv = jnp.dot(vT_sc[key_block(t)], p_sc[slot][...], preferred_element_type=F32)
        acc_sc[...] = a_sc[slot][...] * acc_sc[...] + pv

    assert n_diag in (1, 2)
    scores(0, 0)

    @pl.when(n_blocks == 1)
    def _():
        softmax(0)
        accumulate(0, 0)

    @pl.when(n_blocks >= 2)
    def _():
        scores(1, 1)
        softmax(0)
        n_steady = n_blocks - 2

        def tick_pair(i, carry):
            t = 1 + 2 * i
            softmax(1)
            accumulate(t - 1, 0)
            scores(t + 1, 0)
            softmax(0)
            accumulate(t, 1)
            scores(t + 2, 1)
            return carry

        lax.fori_loop(0, n_steady // 2, tick_pair, 0)
        last = n_blocks - 1

        @pl.when(n_steady % 2 == 1)
        def _():
            accumulate(last - 2, 0)
            scores(last, 0)
            softmax(1)
            accumulate(last - 1, 1)
            softmax(0)
            accumulate(last, 0)

        @pl.when(n_steady % 2 == 0)
        def _():
            accumulate(last - 1, 0)
            softmax(1)
            accumulate(last, 1)

    lp = lam_ref[...]
    lam = (jnp.exp(jnp.sum(lp[0:1] * lp[1:2], axis=-1, keepdims=True))
           - jnp.exp(jnp.sum(lp[2:3] * lp[3:4], axis=-1, keepdims=True)) + lam_init)
    hw = 2 * HEAD_DIM
    o_all = acc_sc[0:hw, :] * (1.0 / acc_sc[hw:hw + 1, :])
    oT = o_all[:, 0:blk] - lam * o_all[:, blk:2 * blk]
    inv_rms = lax.rsqrt(jnp.mean(oT * oT, axis=0, keepdims=True) + SUBLN_EPS)
    oT = oT * inv_rms * (g_ref[...] * (1.0 - lam_init))
    o_ref[0, q_rows, :] = oT.T.astype(o_ref.dtype)


def diff_attention(qkv, lam_params, subln_g, batch, layer_idx, *, blk=512, kblk=512):
    M, W3 = qkv.shape
    W = W3 // 3
    S = M // batch
    heads = W // (2 * HEAD_DIM)
    hw = 2 * HEAD_DIM
    lam_init = 0.8 - 0.6 * float(np.exp(-0.3 * layer_idx))
    view = qkv.reshape(batch, S, W3)
    out = pl.pallas_call(
        functools.partial(_diff_attn_kernel, blk=blk, kblk=kblk, lam_init=lam_init),
        grid=(batch, heads),
        in_specs=[
            pl.BlockSpec((4, HEAD_DIM), lambda b, h: (0, 0)),
            pl.BlockSpec((1, S, hw), lambda b, h: (b, 0, h)),
            pl.BlockSpec((1, S, hw), lambda b, h: (b, 0, heads + h)),
            pl.BlockSpec((1, S, hw), lambda b, h: (b, 0, 2 * heads + h)),
            pl.BlockSpec((hw, 1), lambda b, h: (0, 0)),
        ],
        out_specs=pl.BlockSpec((1, S, hw), lambda b, h: (b, 0, h)),
        out_shape=jax.ShapeDtypeStruct((batch, S, W), BF16),
        scratch_shapes=[
            pltpu.VMEM((hw, 2 * blk), BF16),
            pltpu.VMEM((S // kblk, hw + ONES_ROWS, kblk), BF16),
            pltpu.VMEM((kblk, 2 * blk), F32), pltpu.VMEM((kblk, 2 * blk), F32),
            pltpu.VMEM((kblk, 2 * blk), BF16), pltpu.VMEM((kblk, 2 * blk), BF16),
            pltpu.VMEM((1, 2 * blk), F32), pltpu.VMEM((1, 2 * blk), F32),
            pltpu.VMEM((1, 2 * blk), F32), pltpu.VMEM((1, 2 * blk), F32),
            pltpu.VMEM((1, 2 * blk), F32),
            pltpu.VMEM((hw + ONES_ROWS, 2 * blk), F32),
        ],
        compiler_params=_params("parallel", "parallel"),
        name="diff_attn",
    )(lam_params, view, view, view, subln_g.reshape(hw, 1))
    return out.reshape(M, W)


def _ffn(x, h, w_in, w_out, widx, g_post, g_next):
    act = matmul_swiglu(h, w_in, widx, tm=1024, tn=512)
    return matmul_resid(act, w_out, widx, x, g_post, g_next, 0.5, tm=1024, tk=512)


def kernel(x, positions, norm_g, ffn_in, ffn_out, dil_w_in, dil_w_out, diff_w_in, diff_lambda,
           diff_subln_g, diff_w_out):
    B, S, D = x.shape
    depth = norm_g.shape[0]
    M = B * S
    x = x.reshape(M, D)
    g = norm_g.reshape(depth * 6, 1, D)
    ffn_in, ffn_out, dil_w_in, dil_w_out, diff_w_in, diff_w_out = (
        w.astype(BF16) for w in (ffn_in, ffn_out, dil_w_in, dil_w_out, diff_w_in, diff_w_out))
    tabs = rope_tables(positions)

    h = rms_cast(x, g[0])
    for l in range(depth):
        gl = g[6 * l:6 * l + 6]
        x, h = _ffn(x, h, ffn_in, ffn_out, (l, 0), gl[1], gl[2])
        if l % 2 == 0:
            parts = []
            for grp, (_, dil) in enumerate(DIL_PATTERNS):
                qkv = matmul_rope(h, dil_w_in, (l // 2,), tabs, tm=1024, tn=DIL_GROUP_WIDTH,
                                  part_width=DIL_GROUP_WIDTH, col0=grp, col_stride=DIL_GROUPS, dilation=dil)
                qkv = qkv.reshape(M // (BAND_BLOCK * dil), dil, BAND_BLOCK, 3 * DIL_GROUP_WIDTH)
                parts.append(dilated_group_attention(qkv, grp, dil, B))
            mixed = dilated_mix([p[0] for p in parts], [p[1] for p in parts])
            x, h = matmul_resid(mixed, dil_w_out, (l // 2,), x, gl[3], gl[4], 1.0, tm=1024, tk=768)
        else:
            qkv = matmul_rope(h, diff_w_in, (l // 2,), tabs, tm=1024, tn=1024,
                              part_width=diff_w_in.shape[-1] // 3)
            att = diff_attention(qkv, diff_lambda[l // 2], diff_subln_g[l // 2], B, l)
            x, h = matmul_resid(att, diff_w_out, (l // 2,), x, gl[3], gl[4], 1.0, tm=1024, tk=512)
        g_next = g[6 * l + 6] if l + 1 < depth else None
        x, h = _ffn(x, h, ffn_in, ffn_out, (l, 1), gl[5], g_next)
    return x.reshape(B, S, D)
```

```python
import functools

import numpy as np
import jax
import jax.numpy as jnp
from jax import lax
from jax.experimental import pallas as pl
from jax.experimental.pallas import tpu as pltpu

HEAD_DIM = 128
ROPE_THETA = 500000.0
ROPE_DIM = HEAD_DIM // 4
ROPE_HALF = ROPE_DIM // 2
NORM_EPS = 1e-6
SUBLN_EPS = 1e-5
DIL_PATTERNS = ((128, 1), (512, 4), (2048, 16))
DIL_GROUPS = len(DIL_PATTERNS)
DIL_HEADS_PER_GROUP = 10
DIL_HEADS = DIL_GROUPS * DIL_HEADS_PER_GROUP
DIL_WIDTH = DIL_HEADS * HEAD_DIM
DIL_GROUP_WIDTH = DIL_HEADS_PER_GROUP * HEAD_DIM
BAND_BLOCK = 128
LANES = 128
LN2 = float(np.log(2.0))
Q_PRESCALE = float(HEAD_DIM ** -0.5 / np.log(2.0))
assert all(window // dil == BAND_BLOCK for window, dil in DIL_PATTERNS)

VMEM_LIMIT_BYTES = 56 * 1024 * 1024

F32 = jnp.float32
BF16 = jnp.bfloat16


def _params(*sem):
    return pltpu.CompilerParams(dimension_semantics=sem, vmem_limit_bytes=VMEM_LIMIT_BYTES)


def _rms(x, eps):
    return x * lax.rsqrt(jnp.mean(x * x, axis=-1, keepdims=True) + eps)


def _inv_rms(x):
    return lax.rsqrt(jnp.mean(x * x, axis=-1, keepdims=True) + NORM_EPS)


def _rms_cast_kernel(x_ref, g_ref, o_ref):
    o_ref[...] = (_rms(x_ref[...], NORM_EPS) * g_ref[...]).astype(o_ref.dtype)


def rms_cast(x, g, *, tm=256):
    M, D = x.shape
    return pl.pallas_call(
        _rms_cast_kernel,
        grid=(M // tm,),
        in_specs=[pl.BlockSpec((tm, D), lambda i: (i, 0)), pl.BlockSpec((1, D), lambda i: (0, 0))],
        out_specs=pl.BlockSpec((tm, D), lambda i: (i, 0)),
        out_shape=jax.ShapeDtypeStruct((M, D), BF16),
        compiler_params=_params("parallel"),
        name="rms_cast",
    )(x, g)


def _weight_spec(w, widx, tn, col=lambda j: j):
    K = w.shape[-2]
    return pl.BlockSpec((None,) * len(widx) + (K, tn), lambda i, j: (*widx, 0, col(j)))


RESID_COL_CHUNK = 1024
RESID_ROW_CHUNK = 128


def _mm_resid_kernel(a_ref, w_ref, x_hbm, gp_ref, gn_ref, xo_hbm, *rest, coef, with_h):
    if with_h:
        ho_hbm, acc_sc, x_sc, h_sc, x_sem, xo_sem, h_sem = rest
    else:
        acc_sc, x_sc, x_sem, xo_sem = rest
    k = pl.program_id(1)
    tm = acc_sc.shape[0]
    row0 = pl.multiple_of(pl.program_id(0) * tm, tm)
    x_copy = pltpu.make_async_copy(x_hbm.at[pl.ds(row0, tm)], x_sc, x_sem)

    def partial_products():
        a = a_ref[...]
        for n0 in range(0, acc_sc.shape[1], RESID_COL_CHUNK):
            cols = slice(n0, n0 + RESID_COL_CHUNK)
            yield cols, jnp.dot(a, w_ref[:, cols], preferred_element_type=F32)

    @pl.when(k == 0)
    def _():
        x_copy.start()
        for cols, part in partial_products():
            acc_sc[:, cols] = part

    @pl.when(k > 0)
    def _():
        for cols, part in partial_products():
            acc_sc[:, cols] += part

    @pl.when(k == pl.num_programs(1) - 1)
    def _():
        x_copy.wait()
        gain = coef * gp_ref[...]
        xo_copies, h_copies = [], []
        for c in range(tm // RESID_ROW_CHUNK):
            rows = slice(c * RESID_ROW_CHUNK, (c + 1) * RESID_ROW_CHUNK)
            out_rows = pl.ds(row0 + c * RESID_ROW_CHUNK, RESID_ROW_CHUNK)
            acc_sc[rows, :] = x_sc[rows, :] + acc_sc[rows, :] * _inv_rms(acc_sc[rows, :]) * gain
            xo_copies.append(pltpu.make_async_copy(acc_sc.at[rows], xo_hbm.at[out_rows], xo_sem.at[c]))
            xo_copies[-1].start()
            if with_h:
                slot = c % 2
                if c >= 2:
                    h_copies[c - 2].wait()
                h_sc[slot] = (acc_sc[rows, :] * _inv_rms(acc_sc[rows, :]) * gn_ref[...]).astype(h_sc.dtype)
                h_copies.append(pltpu.make_async_copy(h_sc.at[slot], ho_hbm.at[out_rows], h_sem.at[slot]))
                h_copies[-1].start()
        for cp in h_copies[-2:] + xo_copies:
            cp.wait()


def matmul_resid(a, w, widx, x, g_post, g_next, coef, *, tm, tk):
    M, K = a.shape
    D = w.shape[-1]
    vec = pl.BlockSpec((1, D), lambda i, k: (0, 0))
    hbm = pl.BlockSpec(memory_space=pl.ANY)
    with_h = g_next is not None
    n_chunks = tm // RESID_ROW_CHUNK
    scratch = [pltpu.VMEM((tm, D), F32), pltpu.VMEM((tm, D), F32)]
    sems = [pltpu.SemaphoreType.DMA(()), pltpu.SemaphoreType.DMA((n_chunks,))]
    if with_h:
        scratch.append(pltpu.VMEM((2, RESID_ROW_CHUNK, D), BF16))
        sems.append(pltpu.SemaphoreType.DMA((2,)))
    out = pl.pallas_call(
        functools.partial(_mm_resid_kernel, coef=coef, with_h=with_h),
        grid=(M // tm, K // tk),
        in_specs=[pl.BlockSpec((tm, tk), lambda i, k: (i, k)),
                  pl.BlockSpec((None,) * len(widx) + (tk, D), lambda i, k: (*widx, k, 0)),
                  hbm, vec, vec],
        out_specs=[hbm, hbm] if with_h else [hbm],
        out_shape=[jax.ShapeDtypeStruct((M, D), F32)] + ([jax.ShapeDtypeStruct((M, D), BF16)] if with_h else []),
        scratch_shapes=scratch + sems,
        compiler_params=_params("arbitrary", "arbitrary"),
        name="matmul_resid",
    )(a, w, x, g_post, g_next if with_h else g_post)
    return (out[0], out[1]) if with_h else (out[0], None)


def _mm_swiglu_kernel(a_ref, wg_ref, wu_ref, o_ref):
    a = a_ref[...]
    gate = jnp.dot(a, wg_ref[...], preferred_element_type=F32)
    up = jnp.dot(a, wu_ref[...], preferred_element_type=F32)
    o_ref[...] = (gate * jax.nn.sigmoid(gate) * up).astype(o_ref.dtype)


def matmul_swiglu(a, w_in, widx, *, tm, tn):
    M, K = a.shape
    F = w_in.shape[-1] // 2
    nj = F // tn
    return pl.pallas_call(
        _mm_swiglu_kernel,
        grid=(M // tm, nj),
        in_specs=[
            pl.BlockSpec((tm, K), lambda i, j: (i, 0)),
            _weight_spec(w_in, widx, tn),
            _weight_spec(w_in, widx, tn, lambda j: j + nj),
        ],
        out_specs=pl.BlockSpec((tm, tn), lambda i, j: (i, j)),
        out_shape=jax.ShapeDtypeStruct((M, F), BF16),
        compiler_params=_params("parallel", "arbitrary"),
        name="matmul_swiglu",
    )(a, w_in, w_in)


def _rope_table_kernel(pos_ref, invf_ref, c_ref, s1_ref, s2_ref):
    ang = pos_ref[...].astype(F32) * invf_ref[...]
    lane = lax.broadcasted_iota(jnp.int32, ang.shape, 1)
    cos = jnp.cos(ang)
    sin = jnp.sin(ang)
    c_ref[...] = jnp.where(lane < ROPE_DIM, cos, 1.0)
    s1_ref[...] = jnp.where(lane < ROPE_HALF, -sin, 0.0)
    s2_ref[...] = jnp.where((lane >= ROPE_HALF) & (lane < ROPE_DIM), sin, 0.0)


def rope_tables(positions, *, tm=512):
    M = positions.size
    inv_freq = 1.0 / (ROPE_THETA ** (jnp.arange(0, ROPE_DIM, 2, dtype=F32) / ROPE_DIM))
    invf = jnp.concatenate([inv_freq, inv_freq, jnp.zeros((LANES - ROPE_DIM,), F32)]).reshape(1, LANES)
    tab = jax.ShapeDtypeStruct((M, LANES), F32)
    row = pl.BlockSpec((tm, LANES), lambda i: (i, 0))
    return pl.pallas_call(
        _rope_table_kernel,
        grid=(M // tm,),
        in_specs=[pl.BlockSpec((tm, 1), lambda i: (i, 0)), pl.BlockSpec((1, LANES), lambda i: (0, 0))],
        out_specs=[row, row, row],
        out_shape=[tab, tab, tab],
        compiler_params=_params("parallel"),
        name="rope_tables",
    )(positions.reshape(M, 1), invf)


ROPE_ROW_CHUNK = 256


def _mm_rope_kernel(a_ref, w_ref, c_ref, s1_ref, s2_ref, o_ref, *stage_refs, tiles_per_part, dilation):
    part = pl.program_id(1) // tiles_per_part
    is_rope = part < 2
    mult = jnp.where(part == 0, Q_PRESCALE, 1.0)
    for r0 in range(0, a_ref.shape[0], ROPE_ROW_CHUNK):
        rows = slice(r0, r0 + ROPE_ROW_CHUNK)
        acc = jnp.dot(a_ref[rows, :], w_ref[...], preferred_element_type=F32)
        c = jnp.where(is_rope, c_ref[rows, :] * mult, 1.0)
        s1 = jnp.where(is_rope, s1_ref[rows, :] * mult, 0.0)
        s2 = jnp.where(is_rope, s2_ref[rows, :] * mult, 0.0)
        stage = stage_refs[(r0 // ROPE_ROW_CHUNK) % 2] if dilation > 1 else None
        for h in range(acc.shape[1] // HEAD_DIM):
            sl = slice(h * HEAD_DIM, (h + 1) * HEAD_DIM)
            t = acc[:, sl]
            fwd = pltpu.roll(t, HEAD_DIM - ROPE_HALF, 1)
            bwd = pltpu.roll(t, ROPE_HALF, 1)
            roped = t * c + fwd * s1 + bwd * s2
            if dilation == 1:
                o_ref[rows, sl] = roped.astype(o_ref.dtype)
            else:
                stage[h] = roped
        if dilation > 1:
            n = ROPE_ROW_CHUNK // dilation
            sup, i0 = divmod(r0 // dilation, o_ref.shape[2])
            for h in range(acc.shape[1] // HEAD_DIM):
                sl = slice(h * HEAD_DIM, (h + 1) * HEAD_DIM)
                for r in range(dilation):
                    o_ref[sup, r, i0:i0 + n, sl] = stage[h, pl.ds(r, n, stride=dilation), :].astype(o_ref.dtype)


def matmul_rope(a, w, widx, tabs, *, tm, tn, part_width, col0=0, col_stride=1, dilation=1):
    M, K = a.shape
    tiles_per_part = part_width // tn
    tab = pl.BlockSpec((tm, LANES), lambda i, j: (i, 0))

    def col(j):
        return col0 + (j // tiles_per_part) * col_stride * tiles_per_part + j % tiles_per_part

    if dilation == 1:
        out_spec = pl.BlockSpec((tm, tn), lambda i, j: (i, j))
        out_shape = jax.ShapeDtypeStruct((M, 3 * part_width), BF16)
        scratch = []
    else:
        span = BAND_BLOCK * dilation
        tiles_per_span = max(span // tm, 1)
        out_spec = pl.BlockSpec((max(tm // span, 1), dilation, min(BAND_BLOCK, tm // dilation), tn),
                                lambda i, j: (i // tiles_per_span, 0, i % tiles_per_span, j))
        out_shape = jax.ShapeDtypeStruct((M // span, dilation, BAND_BLOCK, 3 * part_width), BF16)
        scratch = [pltpu.VMEM((tn // HEAD_DIM, ROPE_ROW_CHUNK, HEAD_DIM), F32)] * 2
    return pl.pallas_call(
        functools.partial(_mm_rope_kernel, tiles_per_part=tiles_per_part, dilation=dilation),
        grid=(M // tm, 3 * tiles_per_part),
        in_specs=[pl.BlockSpec((tm, K), lambda i, j: (i, 0)), _weight_spec(w, widx, tn, col), tab, tab, tab],
        out_specs=out_spec,
        out_shape=out_shape,
        scratch_shapes=scratch,
        compiler_params=_params("parallel", "arbitrary"),
        name="matmul_rope",
    )(a, w, *tabs)


_NT = (((1,), (1,)), ((), ()))


DIL_BLOCKS_PER_STEP = 4


def _dil_attn_kernel(q_ref, kp_ref, kc_ref, vp_ref, vc_ref, o_ref, lse_ref):
    qi = lax.broadcasted_iota(jnp.int32, (BAND_BLOCK, 2 * BAND_BLOCK), 0)
    kj = lax.broadcasted_iota(jnp.int32, (BAND_BLOCK, 2 * BAND_BLOCK), 1)
    band = (kj >= qi) & (kj <= qi + BAND_BLOCK)
    lane = lax.broadcasted_iota(jnp.int32, (BAND_BLOCK, LANES), 1)
    for u in range(DIL_BLOCKS_PER_STEP):
        rows = slice(u * BAND_BLOCK, (u + 1) * BAND_BLOCK)
        if u == 0:
            mask = band & ((pl.program_id(2) > 0) | (kj >= BAND_BLOCK))
        else:
            mask = band
        lse_tile = jnp.zeros((BAND_BLOCK, LANES), F32)
        for j in range(DIL_HEADS_PER_GROUP):
            sl = slice(j * HEAD_DIM, (j + 1) * HEAD_DIM)
            k_prev = kp_ref[:, sl] if u == 0 else kc_ref[u - 1, :, sl]
            v_prev = vp_ref[:, sl] if u == 0 else vc_ref[u - 1, :, sl]
            k = jnp.concatenate([k_prev, kc_ref[u, :, sl]], axis=0)
            v = jnp.concatenate([v_prev, vc_ref[u, :, sl]], axis=0)
            s = lax.dot_general(q_ref[u, :, sl], k, _NT, preferred_element_type=F32)
            s = jnp.where(mask, s, -jnp.inf)
            m = jnp.max(s, axis=-1, keepdims=True)
            p = jnp.exp2(s - m)
            den = jnp.sum(p, axis=-1, keepdims=True)
            o = jnp.dot(p.astype(BF16), v, preferred_element_type=F32)
            o_ref[0, rows, sl] = o * (1.0 / den)
            lse_tile = jnp.where(lane == j, m * LN2 + jnp.log(den), lse_tile)
        lse_ref[0, rows, :] = lse_tile


def dilated_group_attention(qkv, group, dilation, batch):
    M = qkv.shape[0] * dilation * BAND_BLOCK
    S = M // batch
    L = S // dilation
    nb = L // BAND_BLOCK
    per = DIL_BLOCKS_PER_STEP
    assert nb % per == 0, (nb, per)
    steps = nb // per

    def spec(part, prev):
        if prev:
            return pl.BlockSpec((None, None, BAND_BLOCK, DIL_GROUP_WIDTH),
                                lambda b, r, n: (b * nb + jnp.maximum(n * per - 1, 0), r, 0, part))
        return pl.BlockSpec((per, None, BAND_BLOCK, DIL_GROUP_WIDTH),
                            lambda b, r, n: (b * steps + n, r, 0, part))

    o, lse = pl.pallas_call(
        _dil_attn_kernel,
        grid=(batch, dilation, steps),
        in_specs=[spec(0, False), spec(1, True), spec(1, False), spec(2, True), spec(2, False)],
        out_specs=[pl.BlockSpec((1, per * BAND_BLOCK, DIL_GROUP_WIDTH), lambda b, r, n: (b, n, r)),
                   pl.BlockSpec((1, per * BAND_BLOCK, LANES), lambda b, r, n: (b, n, r))],
        out_shape=[jax.ShapeDtypeStruct((batch, L, dilation * DIL_GROUP_WIDTH), F32),
                   jax.ShapeDtypeStruct((batch, L, dilation * LANES), F32)],
        compiler_params=_params("parallel", "parallel", "arbitrary"),
        name=f"dil_attn_g{group}",
    )(qkv, qkv, qkv, qkv, qkv)
    return o.reshape(M, DIL_GROUP_WIDTH), lse.reshape(M, LANES)


def _dil_mix_kernel(o0_ref, o1_ref, o2_ref, l0_ref, l1_ref, l2_ref, out_ref):
    lses = [l0_ref[...], l1_ref[...], l2_ref[...]]
    mx = jnp.maximum(jnp.maximum(lses[0], lses[1]), lses[2])
    es = [jnp.exp(l - mx) for l in lses]
    tot = es[0] + es[1] + es[2]
    for g, o_ref in enumerate((o0_ref, o1_ref, o2_ref)):
        alpha = es[g] / tot
        for j in range(DIL_HEADS_PER_GROUP):
            src = slice(j * HEAD_DIM, (j + 1) * HEAD_DIM)
            col = (g * DIL_HEADS_PER_GROUP + j) * HEAD_DIM
            out_ref[:, col:col + HEAD_DIM] = (alpha[:, j:j + 1] * o_ref[:, src]).astype(out_ref.dtype)


def dilated_mix(outs, lses, *, tm=256):
    M = outs[0].shape[0]
    o_spec = pl.BlockSpec((tm, DIL_GROUP_WIDTH), lambda i: (i, 0))
    l_spec = pl.BlockSpec((tm, LANES), lambda i: (i, 0))
    return pl.pallas_call(
        _dil_mix_kernel,
        grid=(M // tm,),
        in_specs=[o_spec] * 3 + [l_spec] * 3,
        out_specs=pl.BlockSpec((tm, DIL_WIDTH), lambda i: (i, 0)),
        out_shape=jax.ShapeDtypeStruct((M, DIL_WIDTH), BF16),
        compiler_params=_params("parallel"),
        name="dil_mix",
    )(*outs, *lses)


ONES_ROWS = 16


def _diff_attn_kernel(lam_ref, q_ref, k_ref, v_ref, g_ref, o_ref, qbd_sc, vT_sc, s0_sc, s1_sc, p0_sc, p1_sc,
                      bm0_sc, bm1_sc, a0_sc, a1_sc, m_sc, acc_sc, *, blk, kblk, lam_init):
    s_sc = (s0_sc, s1_sc)
    p_sc = (p0_sc, p1_sc)
    bm_sc = (bm0_sc, bm1_sc)
    a_sc = (a0_sc, a1_sc)
    seq = v_ref.shape[1]

    hw = v_ref.shape[2]
    for kb in range(seq // kblk):
        vT_sc[kb, 0:hw, :] = v_ref[0, kb * kblk:(kb + 1) * kblk, :].astype(F32).T.astype(BF16)
        vT_sc[kb, hw:hw + ONES_ROWS, :] = jnp.ones((ONES_ROWS, kblk), BF16)
    qbd_sc[...] = jnp.zeros(qbd_sc.shape, BF16)

    def query_block(qi, carry):
        _diff_attn_query_block(qi, lam_ref, q_ref, k_ref, g_ref, o_ref, qbd_sc, vT_sc, s_sc, p_sc, bm_sc, a_sc,
                               m_sc, acc_sc, blk=blk, kblk=kblk, lam_init=lam_init)
        return carry

    lax.fori_loop(0, seq // blk, query_block, 0)


def _diff_attn_query_block(qi, lam_ref, q_ref, k_ref, g_ref, o_ref, qbd_sc, vT_sc, s_sc, p_sc, bm_sc, a_sc,
                           m_sc, acc_sc, *, blk, kblk, lam_init):
    n_diag = blk // kblk
    n_blocks = n_diag * (qi + 1)
    q_rows = pl.ds(pl.multiple_of(qi * blk, blk), blk)

    qT = q_ref[0, q_rows, :].astype(F32).T.astype(BF16)
    qbd_sc[0:HEAD_DIM, 0:blk] = qT[0:HEAD_DIM]
    qbd_sc[HEAD_DIM:2 * HEAD_DIM, blk:2 * blk] = qT[HEAD_DIM:2 * HEAD_DIM]
    m_sc[...] = jnp.full(m_sc.shape, -jnp.inf, F32)
    acc_sc[...] = jnp.zeros(acc_sc.shape, F32)

    def key_block(t):
        return jnp.where(t < n_diag, n_diag * qi + t, t - n_diag)

    def scores(t, slot):
        k0 = pl.multiple_of(key_block(t) * kblk, kblk)
        sT = jnp.dot(k_ref[0, pl.ds(k0, kblk), :], qbd_sc[...], preferred_element_type=F32)
        if isinstance(t, int) and t < n_diag:
            key = t * kblk + lax.broadcasted_iota(jnp.int32, sT.shape, 0)
            qry = lax.broadcasted_iota(jnp.int32, sT.shape, 1) & (blk - 1)
            sT = jnp.where(key <= qry, sT, -jnp.inf)
        s_sc[slot][...] = sT
        bm_sc[slot][...] = jnp.max(sT, axis=0, keepdims=True)

    def softmax(slot):
        for c in range(2):
            cols = slice(c * blk, (c + 1) * blk)
            m_prev = m_sc[:, cols]
            m_new = jnp.maximum(m_prev, bm_sc[slot][:, cols])
            p_sc[slot][:, cols] = jnp.exp2(s_sc[slot][:, cols] - m_new).astype(BF16)
            a_sc[slot][:, cols] = jnp.exp2(m_prev - m_new)
            m_sc[:, cols] = m_new

    def accumulate(t, slot):
        pv = jnp.dot(vT_sc[key_block(t)], p_sc[slot][...], preferred_element_type=F32)
        acc_sc[...] = a_sc[slot][...] * acc_sc[...] + pv

    assert n_diag in (1, 2)
    scores(0, 0)

    @pl.when(n_blocks == 1)
    def _():
        softmax(0)
        accumulate(0, 0)

    @pl.when(n_blocks >= 2)
    def _():
        scores(1, 1)
        softmax(0)
        n_steady = n_blocks - 2

        def tick_pair(i, carry):
            t = 1 + 2 * i
            softmax(1)
            accumulate(t - 1, 0)
            scores(t + 1, 0)
            softmax(0)
            accumulate(t, 1)
            scores(t + 2, 1)
            return carry

        lax.fori_loop(0, n_steady // 2, tick_pair, 0)
        last = n_blocks - 1

        @pl.when(n_steady % 2 == 1)
        def _():
            accumulate(last - 2, 0)
            scores(last, 0)
            softmax(1)
            accumulate(last - 1, 1)
            softmax(0)
            accumulate(last, 0)

        @pl.when(n_steady % 2 == 0)
        def _():
            accumulate(last - 1, 0)
            softmax(1)
            accumulate(last, 1)

    lp = lam_ref[...]
    lam = (jnp.exp(jnp.sum(lp[0:1] * lp[1:2], axis=-1, keepdims=True))
           - jnp.exp(jnp.sum(lp[2:3] * lp[3:4], axis=-1, keepdims=True)) + lam_init)
    hw = 2 * HEAD_DIM
    o_all = acc_sc[0:hw, :] * (1.0 / acc_sc[hw:hw + 1, :])
    oT = o_all[:, 0:blk] - lam * o_all[:, blk:2 * blk]
    inv_rms = lax.rsqrt(jnp.mean(oT * oT, axis=0, keepdims=True) + SUBLN_EPS)
    oT = oT * inv_rms * (g_ref[...] * (1.0 - lam_init))
    o_ref[0, q_rows, :] = oT.T.astype(o_ref.dtype)


def diff_attention(qkv, lam_params, subln_g, batch, layer_idx, *, blk=512, kblk=512):
    M, W3 = qkv.shape
    W = W3 // 3
    S = M // batch
    heads = W // (2 * HEAD_DIM)
    hw = 2 * HEAD_DIM
    lam_init = 0.8 - 0.6 * float(np.exp(-0.3 * layer_idx))
    view = qkv.reshape(batch, S, W3)
    out = pl.pallas_call(
        functools.partial(_diff_attn_kernel, blk=blk, kblk=kblk, lam_init=lam_init),
        grid=(batch, heads),
        in_specs=[
            pl.BlockSpec((4, HEAD_DIM), lambda b, h: (0, 0)),
            pl.BlockSpec((1, S, hw), lambda b, h: (b, 0, h)),
            pl.BlockSpec((1, S, hw), lambda b, h: (b, 0, heads + h)),
            pl.BlockSpec((1, S, hw), lambda b, h: (b, 0, 2 * heads + h)),
            pl.BlockSpec((hw, 1), lambda b, h: (0, 0)),
        ],
        out_specs=pl.BlockSpec((1, S, hw), lambda b, h: (b, 0, h)),
        out_shape=jax.ShapeDtypeStruct((batch, S, W), BF16),
        scratch_shapes=[
            pltpu.VMEM((hw, 2 * blk), BF16),
            pltpu.VMEM((S // kblk, hw + ONES_ROWS, kblk), BF16),
            pltpu.VMEM((kblk, 2 * blk), F32), pltpu.VMEM((kblk, 2 * blk), F32),
            pltpu.VMEM((kblk, 2 * blk), BF16), pltpu.VMEM((kblk, 2 * blk), BF16),
            pltpu.VMEM((1, 2 * blk), F32), pltpu.VMEM((1, 2 * blk), F32),
            pltpu.VMEM((1, 2 * blk), F32), pltpu.VMEM((1, 2 * blk), F32),
            pltpu.VMEM((1, 2 * blk), F32),
            pltpu.VMEM((hw + ONES_ROWS, 2 * blk), F32),
        ],
        compiler_params=_params("parallel", "parallel"),
        name="diff_attn",
    )(lam_params, view, view, view, subln_g.reshape(hw, 1))
    return out.reshape(M, W)


def _ffn(x, h, w_in, w_out, widx, g_post, g_next):
    act = matmul_swiglu(h, w_in, widx, tm=1024, tn=512)
    return matmul_resid(act, w_out, widx, x, g_post, g_next, 0.5, tm=1024, tk=512)


def kernel(x, positions, norm_g, ffn_in, ffn_out, dil_w_in, dil_w_out, diff_w_in, diff_lambda,
           diff_subln_g, diff_w_out):
    B, S, D = x.shape
    depth = norm_g.shape[0]
    M = B * S
    x = x.reshape(M, D)
    g = norm_g.reshape(depth * 6, 1, D)
    ffn_in, ffn_out, dil_w_in, dil_w_out, diff_w_in, diff_w_out = (
        w.astype(BF16) for w in (ffn_in, ffn_out, dil_w_in, dil_w_out, diff_w_in, diff_w_out))
    tabs = rope_tables(positions)

    h = rms_cast(x, g[0])
    for l in range(depth):
        gl = g[6 * l:6 * l + 6]
        x, h = _ffn(x, h, ffn_in, ffn_out, (l, 0), gl[1], gl[2])
        if l % 2 == 0:
            parts = []
            for grp, (_, dil) in enumerate(DIL_PATTERNS):
                qkv = matmul_rope(h, dil_w_in, (l // 2,), tabs, tm=1024, tn=DIL_GROUP_WIDTH,
                                  part_width=DIL_GROUP_WIDTH, col0=grp, col_stride=DIL_GROUPS, dilation=dil)
                qkv = qkv.reshape(M // (BAND_BLOCK * dil), dil, BAND_BLOCK, 3 * DIL_GROUP_WIDTH)
                parts.append(dilated_group_attention(qkv, grp, dil, B))
            mixed = dilated_mix([p[0] for p in parts], [p[1] for p in parts])
            x, h = matmul_resid(mixed, dil_w_out, (l // 2,), x, gl[3], gl[4], 1.0, tm=1024, tk=768)
        else:
            qkv = matmul_rope(h, diff_w_in, (l // 2,), tabs, tm=1024, tn=1024,
                              part_width=diff_w_in.shape[-1] // 3)
            att = diff_attention(qkv, diff_lambda[l // 2], diff_subln_g[l // 2], B, l)
            x, h = matmul_resid(att, diff_w_out, (l // 2,), x, gl[3], gl[4], 1.0, tm=1024, tk=512)
        g_next = g[6 * l + 6] if l + 1 < depth else None
        x, h = _ffn(x, h, ffn_in, ffn_out, (l, 1), gl[5], g_next)
    return x.reshape(B, S, D)
```
